```python
import jax, jax.numpy as jnp
from jax import lax
import numpy as np

D_MODEL = 1024
BATCH = 2
SEQ = 8192
DEPTH = 4

N_MIXERS = 4
ALPHA = (2 * DEPTH) ** 0.25
BETA = (8 * DEPTH) ** -0.25
LN_EPS = 1e-5

SHORT_CONV_WIDTH = 3

CHUNK = 128
GMLP_FFN = 6 * D_MODEL
GMLP_HALF = GMLP_FFN // 2
GMLP_HEADS = 8

POOL_WINDOWS = (2, 4, 8, 16)
POOL_GROUPS = len(POOL_WINDOWS)
POOL_GW = D_MODEL // POOL_GROUPS

CONF_CONV_WIDTH = 31

N_EXPERTS = 64
TOP_K = 8
N_EXPERT_GROUPS = 8
TOPK_GROUPS = 4
EXPERT_HIDDEN = 256
SHARED_HIDDEN = 256
ROUTED_SCALE = 2.5
MOE_BLOCK = 128

kernel_name = "hybrid_conv_gmlp_pool_conformer_moe"


def layer_norm(x, g, b):
    xf = x.astype(jnp.float32)
    mu = jnp.mean(xf, axis=-1, keepdims=True)
    var = jnp.mean(jnp.square(xf - mu), axis=-1, keepdims=True)
    return ((xf - mu) * lax.rsqrt(var + LN_EPS) * g + b).astype(x.dtype)


def causal_depthwise_conv(x, w):
    width, ch = w.shape
    return lax.conv_general_dilated(
        x, w[:, None, :].astype(x.dtype), window_strides=(1,), padding=[(width - 1, 0)],
        dimension_numbers=("NWC", "WIO", "NWC"), feature_group_count=ch)


def gated_short_conv(x, w_in, conv_w, w_out):
    b_gate, c_gate, xt = jnp.split(x @ w_in, 3, axis=-1)
    y = b_gate * causal_depthwise_conv(c_gate * xt, conv_w)
    return y @ w_out


def chunked_spatial_gating(x, w_in, v_ln_g, v_ln_b, w_s, s_bias, w_out):
    bsz, seq, _ = x.shape
    u, v = jnp.split(jax.nn.gelu(x @ w_in), 2, axis=-1)
    v = layer_norm(v, v_ln_g, v_ln_b)
    v = v.reshape(bsz, seq // CHUNK, CHUNK, GMLP_HEADS, GMLP_HALF // GMLP_HEADS)
    w_causal = jnp.tril(w_s)
    s = jnp.einsum("hts,bcshd->bcthd", w_causal, v) + s_bias.T[:, :, None]
    return (u * s.reshape(bsz, seq, GMLP_HALF)) @ w_out


def multiscale_pool(x, w_in, w_grp, scale, w_out):
    bsz, seq, d = x.shape
    h = x @ w_in
    csum = jnp.cumsum(h.astype(jnp.float32), axis=1)
    pos = jnp.arange(1, seq + 1, dtype=jnp.float32)
    outs = []
    for g, win in enumerate(POOL_WINDOWS):
        c = csum[..., g * POOL_GW:(g + 1) * POOL_GW]
        lag = jnp.pad(c, ((0, 0), (win, 0), (0, 0)))[:, :seq]
        mean = (c - lag) / jnp.minimum(pos, float(win))[None, :, None]
        outs.append(mean.astype(h.dtype) - h[..., g * POOL_GW:(g + 1) * POOL_GW])
    p = jnp.stack(outs, axis=2)
    y = jnp.einsum("btgc,gce->btge", p, w_grp).reshape(bsz, seq, d) * scale
    return y @ w_out


def conformer_conv(x, w_in, conv_w, conv_b, ln_g, ln_b, w_out):
    a, gate = jnp.split(x @ w_in, 2, axis=-1)
    h = a * jax.nn.sigmoid(gate)
    h = causal_depthwise_conv(h, conv_w) + conv_b
    h = jax.nn.silu(layer_norm(h, ln_g, ln_b))
    return h @ w_out


def moe(x, router_w, router_bias, w_gate, w_up, w_down, sh_gate, sh_up, sh_down):
    bsz, seq, d = x.shape
    xt = x.reshape(-1, d)
    n_tok = xt.shape[0]
    scores = jax.nn.sigmoid((xt @ router_w).astype(jnp.float32))
    biased = scores + router_bias.astype(jnp.float32)
    per_grp = N_EXPERTS // N_EXPERT_GROUPS
    grp_score = lax.top_k(biased.reshape(n_tok, N_EXPERT_GROUPS, per_grp), 2)[0].sum(-1)
    _, top_grp = lax.top_k(grp_score, TOPK_GROUPS)
    grp_mask = jax.nn.one_hot(top_grp, N_EXPERT_GROUPS).sum(1) > 0
    masked = jnp.where(jnp.repeat(grp_mask, per_grp, axis=1), biased, -jnp.inf)
    _, top_e = lax.top_k(masked, TOP_K)
    gate = jnp.take_along_axis(scores, top_e, axis=1)
    gate = gate / jnp.sum(gate, axis=-1, keepdims=True) * ROUTED_SCALE

    n_assign = n_tok * TOP_K
    flat_e = top_e.reshape(-1)
    flat_tok = jnp.broadcast_to(jnp.arange(n_tok, dtype=jnp.int32)[:, None], (n_tok, TOP_K)).reshape(-1)
    flat_w = gate.reshape(-1).astype(x.dtype)
    order = jnp.argsort(flat_e)
    se, st, sw = flat_e[order], flat_tok[order], flat_w[order]
    counts = jnp.bincount(flat_e, length=N_EXPERTS)
    padded = (counts + MOE_BLOCK - 1) // MOE_BLOCK * MOE_BLOCK
    pad_end = jnp.cumsum(padded)
    pad_start = pad_end - padded
    start = jnp.cumsum(counts) - counts
    dest = pad_start[se] + jnp.arange(n_assign) - start[se]
    n_blocks = (n_assign + N_EXPERTS * (MOE_BLOCK - 1) + MOE_BLOCK - 1) // MOE_BLOCK
    n_rows = n_blocks * MOE_BLOCK
    row_tok = jnp.full((n_rows,), n_tok, jnp.int32).at[dest].set(st)
    row_w = jnp.zeros((n_rows,), x.dtype).at[dest].set(sw)
    block_e = jnp.minimum(
        jnp.searchsorted(pad_end, jnp.arange(n_blocks) * MOE_BLOCK, side="right"), N_EXPERTS - 1)
    x_pad = jnp.concatenate([xt, jnp.zeros((1, d), xt.dtype)], axis=0)

    def run_block(args):
        tok, e, wt = args
        h = x_pad[tok]
        y = (jax.nn.silu(h @ w_gate[e]) * (h @ w_up[e])) @ w_down[e]
        return y * wt[:, None]

    y = lax.map(run_block, (row_tok.reshape(n_blocks, MOE_BLOCK), block_e,
                            row_w.reshape(n_blocks, MOE_BLOCK)))
    routed = jax.ops.segment_sum(y.reshape(n_rows, d), row_tok, num_segments=n_tok + 1)[:n_tok]
    shared = (jax.nn.silu(xt @ sh_gate) * (xt @ sh_up)) @ sh_down
    return (routed + shared).reshape(bsz, seq, d)


def setup_inputs(seed: int = 0) -> dict:
    key = jax.random.key(seed)
    ks = iter(jax.random.split(key, 48))

    def nrm(shape, scale):
        return jax.random.normal(next(ks), shape, jnp.float32) * scale

    def gain(shape):
        return 1.0 + 0.05 * jax.random.normal(next(ks), shape, jnp.float32)

    def bias(shape):
        return 0.02 * jax.random.normal(next(ks), shape, jnp.float32)

    n_a, n_b, n_c, n_d = (len(range(m, DEPTH, N_MIXERS)) for m in range(N_MIXERS))
    D = D_MODEL
    return {
        "x": nrm((BATCH, SEQ, D), 1.0),
        "a_w_in": nrm((n_a, D, 3 * D), D ** -0.5),
        "a_conv_w": nrm((n_a, SHORT_CONV_WIDTH, D), SHORT_CONV_WIDTH ** -0.5),
        "a_w_out": nrm((n_a, D, D), BETA * D ** -0.5),
        "b_w_in": nrm((n_b, D, GMLP_FFN), D ** -0.5),
        "b_v_ln_g": gain((n_b, GMLP_HALF)),
        "b_v_ln_b": bias((n_b, GMLP_HALF)),
        "b_w_s": nrm((n_b, GMLP_HEADS, CHUNK, CHUNK), CHUNK ** -0.5),
        "b_s_bias": gain((n_b, GMLP_HEADS, CHUNK)),
        "b_w_out": nrm((n_b, GMLP_HALF, D), BETA * GMLP_HALF ** -0.5),
        "c_w_in": nrm((n_c, D, D), D ** -0.5),
        "c_w_grp": nrm((n_c, POOL_GROUPS, POOL_GW, POOL_GW), POOL_GW ** -0.5),
        "c_scale": gain((n_c, D)),
        "c_w_out": nrm((n_c, D, D), BETA * D ** -0.5),
        "d_w_in": nrm((n_d, D, 2 * D), D ** -0.5),
        "d_conv_w": nrm((n_d, CONF_CONV_WIDTH, D), CONF_CONV_WIDTH ** -0.5),
        "d_conv_b": bias((n_d, D)),
        "d_ln_g": gain((n_d, D)),
        "d_ln_b": bias((n_d, D)),
        "d_w_out": nrm((n_d, D, D), BETA * D ** -0.5),
        "ln1_g": gain((DEPTH, D)),
        "ln1_b": bias((DEPTH, D)),
        "ln2_g": gain((DEPTH, D)),
        "ln2_b": bias((DEPTH, D)),
        "router_w": nrm((DEPTH, D, N_EXPERTS), D ** -0.5),
        "router_bias": nrm((DEPTH, N_EXPERTS), 0.01),
        "exp_w_gate": nrm((DEPTH, N_EXPERTS, D, EXPERT_HIDDEN), D ** -0.5),
        "exp_w_up": nrm((DEPTH, N_EXPERTS, D, EXPERT_HIDDEN), D ** -0.5),
        "exp_w_down": nrm((DEPTH, N_EXPERTS, EXPERT_HIDDEN, D), BETA * EXPERT_HIDDEN ** -0.5),
        "sh_w_gate": nrm((DEPTH, D, SHARED_HIDDEN), D ** -0.5),
        "sh_w_up": nrm((DEPTH, D, SHARED_HIDDEN), D ** -0.5),
        "sh_w_down": nrm((DEPTH, SHARED_HIDDEN, D), BETA * SHARED_HIDDEN ** -0.5),
    }


def reference(x, a_w_in, a_conv_w, a_w_out,
              b_w_in, b_v_ln_g, b_v_ln_b, b_w_s, b_s_bias, b_w_out,
              c_w_in, c_w_grp, c_scale, c_w_out,
              d_w_in, d_conv_w, d_conv_b, d_ln_g, d_ln_b, d_w_out,
              ln1_g, ln1_b, ln2_g, ln2_b,
              router_w, router_bias, exp_w_gate, exp_w_up, exp_w_down,
              sh_w_gate, sh_w_up, sh_w_down):
    for i in range(DEPTH):
        m, j = i % N_MIXERS, i // N_MIXERS
        if m == 0:
            h = gated_short_conv(x, a_w_in[j], a_conv_w[j], a_w_out[j])
        elif m == 1:
            h = chunked_spatial_gating(x, b_w_in[j], b_v_ln_g[j], b_v_ln_b[j], b_w_s[j],
                                       b_s_bias[j], b_w_out[j])
        elif m == 2:
            h = multiscale_pool(x, c_w_in[j], c_w_grp[j], c_scale[j], c_w_out[j])
        else:
            h = conformer_conv(x, d_w_in[j], d_conv_w[j], d_conv_b[j], d_ln_g[j], d_ln_b[j],
                               d_w_out[j])
        x = layer_norm(ALPHA * x + h, ln1_g[i], ln1_b[i])
        f = moe(x, router_w[i], router_bias[i], exp_w_gate[i], exp_w_up[i], exp_w_down[i],
                sh_w_gate[i], sh_w_up[i], sh_w_down[i])
        x = layer_norm(ALPHA * x + f, ln2_g[i], ln2_b[i])
    return x
```

```python
import functools

import jax
import jax.numpy as jnp
from jax import lax
from jax.experimental import pallas as pl
from jax.experimental.pallas import tpu as pltpu

LN_EPS = 1e-5
CHUNK = 128
GMLP_HEADS = 8
POOL_WINDOWS = (2, 4, 8, 16)
N_EXPERT_GROUPS = 8
TOPK_GROUPS = 4
TOP_K = 8
ROUTED_SCALE = 2.5
ROW_BLOCK = 256
LANES = 128
VMEM_LIMIT = 56 * 1024 * 1024

_BF16 = jnp.bfloat16
_F32 = jnp.float32


def _dot(a, b):
    return jnp.dot(a, b, preferred_element_type=_F32)


def _layer_norm(z, g, b):
    mu = jnp.mean(z, axis=-1, keepdims=True)
    zc = z - mu
    var = jnp.mean(zc * zc, axis=-1, keepdims=True)
    return zc * lax.rsqrt(var + LN_EPS) * g + b


def _pack_pair(lo, hi):
    lo_bits = lax.bitcast_convert_type(lo.astype(_BF16).astype(_F32), jnp.uint32)
    hi_bits = lax.bitcast_convert_type(hi.astype(_BF16).astype(_F32), jnp.uint32)
    return lax.shift_right_logical(lo_bits, jnp.uint32(16)) | (hi_bits & jnp.uint32(0xFFFF0000))


def _unpack_pair(p):
    lo = lax.bitcast_convert_type(lax.shift_left(p, jnp.uint32(16)), _F32)
    hi = lax.bitcast_convert_type(p & jnp.uint32(0xFFFF0000), _F32)
    return lo, hi


def _store_residual_ln(x, y, g_ref, b_ref, alpha, o_ref, op_ref):
    x1 = _layer_norm(alpha * x + y, g_ref[...], b_ref[...])
    o_ref[...] = x1
    half = x1.shape[-1] // 2
    op_ref[...] = _pack_pair(x1[:, :half], x1[:, half:])


def _params(n_axes=1):
    return pltpu.CompilerParams(dimension_semantics=("arbitrary",) * n_axes, vmem_limit_bytes=VMEM_LIMIT)


def _full(shape):
    nd = len(shape)
    return pl.BlockSpec(shape, lambda i, _nd=nd: (0,) * _nd, pipeline_mode=pl.Buffered(1))


def _rows(tm, width):
    return pl.BlockSpec((tm, width), lambda i: (i, 0))


def _mixer_out_shapes(n, d):
    return (jax.ShapeDtypeStruct((n, d), _F32), jax.ShapeDtypeStruct((n, d // 2), jnp.uint32))


def _mixer_a_kernel(x_ref, win_ref, cw_ref, wout_ref, g_ref, b_ref, o_ref, op_ref, zext_ref, *,
                    tm, d, tiles_per_seq, alpha, halo):
    i = pl.program_id(0)
    x = x_ref[...]
    xb = x.astype(_BF16)
    b_gate = _dot(xb, win_ref[:, 0:d])
    z = _dot(xb, win_ref[:, d:2 * d]) * _dot(xb, win_ref[:, 2 * d:3 * d])

    @pl.when(i % tiles_per_seq == 0)
    def _():
        zext_ref[0:halo, :] = jnp.zeros((halo, d), _F32)

    @pl.when(i % tiles_per_seq != 0)
    def _():
        zext_ref[0:halo, :] = zext_ref[tm:tm + halo, :]

    zext_ref[halo:halo + tm, :] = z
    width = cw_ref.shape[0]
    conv = cw_ref[width - 1:width, :] * z
    for k in range(width - 1):
        off = halo - (width - 1) + k
        conv = conv + cw_ref[k:k + 1, :] * zext_ref[off:off + tm, :]
    y = _dot((b_gate * conv).astype(_BF16), wout_ref[...])
    _store_residual_ln(x, y, g_ref, b_ref, alpha, o_ref, op_ref)


def _mixer_a(x2, w_in, conv_w, w_out, ln_g, ln_b, *, seq, alpha, tm=512):
    n, d = x2.shape
    tm = min(tm, seq)
    halo = 8
    kern = functools.partial(_mixer_a_kernel, tm=tm, d=d, tiles_per_seq=seq // tm, alpha=alpha, halo=halo)
    return pl.pallas_call(
        kern,
        grid=(n // tm,),
        in_specs=[_rows(tm, d), _full(w_in.shape), _full(conv_w.shape), _full(w_out.shape),
                  _full((1, d)), _full((1, d))],
        out_specs=(_rows(tm, d), _rows(tm, d // 2)),
        out_shape=_mixer_out_shapes(n, d),
        scratch_shapes=[pltpu.VMEM((tm + halo, d), _F32)],
        compiler_params=_params(),
        name="mixer_a",
    )(x2, w_in.astype(_BF16), conv_w, w_out.astype(_BF16), ln_g.reshape(1, d), ln_b.reshape(1, d))


def _mixer_b_kernel(x_ref, win_ref, vg_ref, vb_ref, ws_ref, sbt_ref, wout_ref, g_ref, b_ref, o_ref, op_ref,
                    gated_ref, *, tm, half, heads, alpha):
    x = x_ref[...]
    xb = x.astype(_BF16)
    u = jax.nn.gelu(_dot(xb, win_ref[:, 0:half]))
    v = jax.nn.gelu(_dot(xb, win_ref[:, half:2 * half]))
    vn = _layer_norm(v, vg_ref[...], vb_ref[...]).astype(_BF16)
    hd = half // heads
    row = lax.broadcasted_iota(jnp.int32, (CHUNK, CHUNK), 0)
    col = lax.broadcasted_iota(jnp.int32, (CHUNK, CHUNK), 1)
    causal = row >= col
    for h in range(heads):
        w_c = jnp.where(causal, ws_ref[h], 0.0).astype(_BF16)
        bias = sbt_ref[:, h:h + 1]
        for c in range(tm // CHUNK):
            rs = slice(c * CHUNK, (c + 1) * CHUNK)
            cs = slice(h * hd, (h + 1) * hd)
            s = _dot(w_c, vn[rs, cs]) + bias
            gated_ref[rs, cs] = (u[rs, cs] * s).astype(_BF16)
    y = _dot(gated_ref[...], wout_ref[...])
    _store_residual_ln(x, y, g_ref, b_ref, alpha, o_ref, op_ref)


def _mixer_b(x2, w_in, v_ln_g, v_ln_b, w_s, s_bias, w_out, ln_g, ln_b, *, alpha, tm=256):
    n, d = x2.shape
    half = w_in.shape[1] // 2
    heads = w_s.shape[0]
    kern = functools.partial(_mixer_b_kernel, tm=tm, half=half, heads=heads, alpha=alpha)
    return pl.pallas_call(
        kern,
        grid=(n // tm,),
        in_specs=[_rows(tm, d), _full(w_in.shape), _full((1, half)), _full((1, half)), _full(w_s.shape),
                  _full((CHUNK, heads)), _full(w_out.shape), _full((1, d)), _full((1, d))],
        out_specs=(_rows(tm, d), _rows(tm, d // 2)),
        out_shape=_mixer_out_shapes(n, d),
        scratch_shapes=[pltpu.VMEM((tm, half), _BF16)],
        compiler_params=_params(),
        name="mixer_b",
    )(x2, w_in.astype(_BF16), v_ln_g.reshape(1, half), v_ln_b.reshape(1, half), w_s, s_bias.T,
      w_out.astype(_BF16), ln_g.reshape(1, d), ln_b.reshape(1, d))


def _mixer_c_kernel(x_ref, win_ref, wgrp_ref, scale_ref, wout_ref, g_ref, b_ref, o_ref, op_ref, hext_ref, *,
                    tm, d, tiles_per_seq, alpha, halo):
    i = pl.program_id(0)
    x = x_ref[...]
    h = _dot(x.astype(_BF16), win_ref[...])

    @pl.when(i % tiles_per_seq == 0)
    def _():
        hext_ref[0:halo, :] = jnp.zeros((halo, d), _F32)

    @pl.when(i % tiles_per_seq != 0)
    def _():
        hext_ref[0:halo, :] = hext_ref[tm:tm + halo, :]

    hext_ref[halo:halo + tm, :] = h
    pos = (i % tiles_per_seq) * tm + lax.broadcasted_iota(jnp.int32, (tm, 1), 0) + 1
    gw = d // len(POOL_WINDOWS)
    y = jnp.zeros((tm, d), _F32)
    for g, win in enumerate(POOL_WINDOWS):
        cs = slice(g * gw, (g + 1) * gw)
        acc = h[:, cs]
        for j in range(1, win):
            acc = acc + hext_ref[halo - j:halo - j + tm, cs]
        mean = acc / jnp.minimum(pos, win).astype(_F32)
        p = mean - h[:, cs]
        yg = _dot(p.astype(_BF16), wgrp_ref[g]) * scale_ref[:, cs]
        y = y + _dot(yg.astype(_BF16), wout_ref[cs, :])
    _store_residual_ln(x, y, g_ref, b_ref, alpha, o_ref, op_ref)


def _mixer_c(x2, w_in, w_grp, scale, w_out, ln_g, ln_b, *, seq, alpha, tm=512):
    n, d = x2.shape
    tm = min(tm, seq)
    halo = 16
    assert max(POOL_WINDOWS) <= halo
    kern = functools.partial(_mixer_c_kernel, tm=tm, d=d, tiles_per_seq=seq // tm, alpha=alpha, halo=halo)
    return pl.pallas_call(
        kern,
        grid=(n // tm,),
        in_specs=[_rows(tm, d), _full(w_in.shape), _full(w_grp.shape), _full((1, d)), _full(w_out.shape),
                  _full((1, d)), _full((1, d))],
        out_specs=(_rows(tm, d), _rows(tm, d // 2)),
        out_shape=_mixer_out_shapes(n, d),
        scratch_shapes=[pltpu.VMEM((tm + halo, d), _F32)],
        compiler_params=_params(),
        name="mixer_c",
    )(x2, w_in.astype(_BF16), w_grp.astype(_BF16), scale.reshape(1, d), w_out.astype(_BF16),
      ln_g.reshape(1, d), ln_b.reshape(1, d))


def _mixer_d_kernel(x_ref, win_ref, cw_ref, cb_ref, cg_ref, cbeta_ref, wout_ref, g_ref, b_ref, o_ref, op_ref,
                    hext_ref, conv_ref, *, tm, d, tiles_per_seq, alpha, halo, row_chunk, lane_chunk):
    i = pl.program_id(0)
    x = x_ref[...]
    xb = x.astype(_BF16)
    h = _dot(xb, win_ref[:, 0:d]) * jax.nn.sigmoid(_dot(xb, win_ref[:, d:2 * d]))

    @pl.when(i % tiles_per_seq == 0)
    def _():
        hext_ref[0:halo, :] = jnp.zeros((halo, d), _F32)

    @pl.when(i % tiles_per_seq != 0)
    def _():
        hext_ref[0:halo, :] = hext_ref[tm:tm + halo, :]

    hext_ref[halo:halo + tm, :] = h
    width = cw_ref.shape[0]
    for lc in range(d // lane_chunk):
        cs = slice(lc * lane_chunk, (lc + 1) * lane_chunk)
        taps = [cw_ref[k:k + 1, cs] for k in range(width)]
        bias = cb_ref[:, cs]
        for rc in range(tm // row_chunk):
            r0 = rc * row_chunk
            acc = jnp.zeros((row_chunk, lane_chunk), _F32) + bias
            for k in range(width):
                off = halo - (width - 1) + k + r0
                acc = acc + taps[k] * hext_ref[off:off + row_chunk, cs]
            conv_ref[r0:r0 + row_chunk, cs] = acc
    c = _layer_norm(conv_ref[...], cg_ref[...], cbeta_ref[...])
    y = _dot(jax.nn.silu(c).astype(_BF16), wout_ref[...])
    _store_residual_ln(x, y, g_ref, b_ref, alpha, o_ref, op_ref)


def _mixer_d(x2, w_in, conv_w, conv_b, c_ln_g, c_ln_b, w_out, ln_g, ln_b, *, seq, alpha, tm=256):
    n, d = x2.shape
    tm = min(tm, seq)
    halo = 32
    assert conv_w.shape[0] - 1 <= halo
    kern = functools.partial(_mixer_d_kernel, tm=tm, d=d, tiles_per_seq=seq // tm, alpha=alpha, halo=halo,
                             row_chunk=32, lane_chunk=256)
    return pl.pallas_call(
        kern,
        grid=(n // tm,),
        in_specs=[_rows(tm, d), _full(w_in.shape), _full(conv_w.shape), _full((1, d)), _full((1, d)),
                  _full((1, d)), _full(w_out.shape), _full((1, d)), _full((1, d))],
        out_specs=(_rows(tm, d), _rows(tm, d // 2)),
        out_shape=_mixer_out_shapes(n, d),
        scratch_shapes=[pltpu.VMEM((tm + halo, d), _F32), pltpu.VMEM((tm, d), _F32)],
        compiler_params=_params(),
        name="mixer_d",
    )(x2, w_in.astype(_BF16), conv_w, conv_b.reshape(1, d), c_ln_g.reshape(1, d), c_ln_b.reshape(1, d),
      w_out.astype(_BF16), ln_g.reshape(1, d), ln_b.reshape(1, d))


def _router_kernel(x_ref, wt_ref, bias_ref, eidx_ref, gate_ref, rank_ref, cnt_ref,
                   carry_ref, gs_ref, masked_ref, graw_ref, *, tm, n_exp):
    i = pl.program_id(0)

    @pl.when(i == 0)
    def _():
        carry_ref[...] = jnp.zeros_like(carry_ref)

    logits = lax.dot_general(wt_ref[...], x_ref[...], (((1,), (1,)), ((), ())),
                             precision=lax.Precision.HIGHEST, preferred_element_type=_F32)
    scores = jax.nn.sigmoid(logits)
    biased = scores + bias_ref[...]
    per = n_exp // N_EXPERT_GROUPS
    neg = -jnp.inf
    sub = lax.broadcasted_iota(jnp.int32, (per, tm), 0)
    for g in range(N_EXPERT_GROUPS):
        bg = biased[g * per:(g + 1) * per, :]
        m1 = jnp.max(bg, axis=0, keepdims=True)
        first = jnp.min(jnp.where(bg == m1, sub, per), axis=0, keepdims=True)
        m2 = jnp.max(jnp.where(sub == first, neg, bg), axis=0, keepdims=True)
        gs_ref[g:g + 1, :] = m1 + m2

    giota = lax.broadcasted_iota(jnp.int32, (N_EXPERT_GROUPS, tm), 0)
    cur = gs_ref[...]
    gsel = jnp.zeros((N_EXPERT_GROUPS, tm), jnp.bool_)
    for _ in range(TOPK_GROUPS):
        m = jnp.max(cur, axis=0, keepdims=True)
        gi = jnp.min(jnp.where(cur == m, giota, N_EXPERT_GROUPS), axis=0, keepdims=True)
        pick = giota == gi
        gsel = gsel | pick
        cur = jnp.where(pick, neg, cur)
    gs_ref[...] = gsel.astype(_F32)
    for g in range(N_EXPERT_GROUPS):
        keep = gs_ref[g:g + 1, :] > 0.5
        masked_ref[g * per:(g + 1) * per, :] = jnp.where(keep, biased[g * per:(g + 1) * per, :], neg)

    eiota = lax.broadcasted_iota(jnp.int32, (n_exp, tm), 0)
    cur = masked_ref[...]
    sel = jnp.zeros((n_exp, tm), jnp.bool_)
    for k in range(TOP_K):
        m = jnp.max(cur, axis=0, keepdims=True)
        ei = jnp.min(jnp.where(cur == m, eiota, n_exp), axis=0, keepdims=True)
        pick = eiota == ei
        eidx_ref[k:k + 1, :] = ei
        graw_ref[k:k + 1, :] = jnp.sum(jnp.where(pick, scores, 0.0), axis=0, keepdims=True)
        sel = sel | pick
        cur = jnp.where(pick, neg, cur)
    graw = graw_ref[...]
    gate_ref[...] = graw / jnp.sum(graw, axis=0, keepdims=True) * ROUTED_SCALE

    sel_b = sel.astype(_F32).astype(_BF16)
    before = (lax.broadcasted_iota(jnp.int32, (tm, tm), 0)
              < lax.broadcasted_iota(jnp.int32, (tm, tm), 1)).astype(_F32).astype(_BF16)
    rank_all = _dot(sel_b, before) + carry_ref[:, 0:1]
    for k in range(TOP_K):
        pick = eiota == eidx_ref[k:k + 1, :]
        rank_ref[k:k + 1, :] = jnp.sum(jnp.where(pick, rank_all, 0.0), axis=0, keepdims=True).astype(jnp.int32)
    carry_ref[...] = carry_ref[...] + jnp.sum(sel.astype(_F32), axis=1, keepdims=True)
    cnt_ref[...] = carry_ref[...]


def _router(x1, router_w, router_bias, *, tm=512):
    n, d = x1.shape
    n_exp = router_w.shape[1]
    tm = min(tm, n)
    kern = functools.partial(_router_kernel, tm=tm, n_exp=n_exp)
    tok = pl.BlockSpec((TOP_K, tm), lambda i: (0, i))
    return pl.pallas_call(
        kern,
        grid=(n // tm,),
        in_specs=[_rows(tm, d), _full((n_exp, d)), _full((n_exp, 1))],
        out_specs=(tok, tok, tok, pl.BlockSpec((n_exp, LANES), lambda i: (0, 0))),
        out_shape=(jax.ShapeDtypeStruct((TOP_K, n), jnp.int32), jax.ShapeDtypeStruct((TOP_K, n), _F32),
                   jax.ShapeDtypeStruct((TOP_K, n), jnp.int32), jax.ShapeDtypeStruct((n_exp, LANES), _F32)),
        scratch_shapes=[pltpu.VMEM((n_exp, LANES), _F32), pltpu.VMEM((N_EXPERT_GROUPS, tm), _F32),
                        pltpu.VMEM((n_exp, tm), _F32), pltpu.VMEM((TOP_K, tm), _F32)],
        compiler_params=_params(),
        name="moe_router",
    )(x1, router_w.T, router_bias.reshape(n_exp, 1))


def _tables_kernel(cnt_ref, eidx_ref, rank_ref, dest_ref, blk_ref, *, n_exp, n_blk_pad):
    counts = cnt_ref[...].astype(jnp.int32)
    shift = ROW_BLOCK.bit_length() - 1
    padded = lax.shift_left(lax.shift_right_logical(counts + (ROW_BLOCK - 1), shift), shift)
    lower = (lax.broadcasted_iota(jnp.int32, (n_exp, n_exp), 1)
             < lax.broadcasted_iota(jnp.int32, (n_exp, n_exp), 0)).astype(_F32)
    pad_start = jnp.dot(lower, padded.astype(_F32), precision=lax.Precision.HIGHEST,
                        preferred_element_type=_F32).astype(jnp.int32)
    pad_end = pad_start + padded
    eidx = eidx_ref[...]
    dest = rank_ref[...]
    for e in range(n_exp):
        dest = dest + jnp.where(eidx == e, pad_start[e:e + 1, 0:1], 0)
    dest_ref[...] = dest

    @pl.when(pl.program_id(0) == 0)
    def _():
        bstart = lax.broadcasted_iota(jnp.int32, (1, n_blk_pad), 1) * ROW_BLOCK
        blk_e = jnp.zeros((1, n_blk_pad), jnp.int32)
        for e in range(n_exp):
            blk_e = blk_e + (pad_end[e:e + 1, 0:1] <= bstart).astype(jnp.int32)
        blk_e = jnp.minimum(blk_e, n_exp - 1)
        used_end = pad_start + counts
        valid = jnp.zeros((1, n_blk_pad), jnp.int32)
        for e in range(n_exp):
            valid = valid + jnp.where(blk_e == e, used_end[e:e + 1, 0:1], 0)
        valid = jnp.clip(valid - bstart, 0, ROW_BLOCK)
        total = pad_end[n_exp - 1:n_exp, 0:1]
        last_active = lax.shift_right_logical(total, shift) - 1
        blk_src = jnp.minimum(lax.broadcasted_iota(jnp.int32, (1, n_blk_pad), 1), last_active)
        blk_ref[0:1, :] = blk_e
        blk_ref[1:2, :] = valid
        blk_ref[2:3, :] = blk_src


def _tables(cnt, eidx, rank, *, n_blk, tm=2048):
    n = eidx.shape[1]
    n_exp = cnt.shape[0]
    tm = min(tm, n)
    n_blk_pad = pl.cdiv(n_blk, LANES) * LANES
    kern = functools.partial(_tables_kernel, n_exp=n_exp, n_blk_pad=n_blk_pad)
    tok = pl.BlockSpec((TOP_K, tm), lambda i: (0, i))
    return pl.pallas_call(
        kern,
        grid=(n // tm,),
        in_specs=[_full((n_exp, LANES)), tok, tok],
        out_specs=(tok, pl.BlockSpec((8, n_blk_pad), lambda i: (0, 0))),
        out_shape=(jax.ShapeDtypeStruct((TOP_K, n), jnp.int32), jax.ShapeDtypeStruct((8, n_blk_pad), jnp.int32)),
        compiler_params=_params(),
        name="moe_tables",
    )(cnt, eidx, rank)


def _ffn_kernel(blk_ref, xg_ref, wg_ref, wu_ref, wd_ref, y_ref, wg_s, wu_s, wd_s, *, half):
    b = pl.program_id(0)
    e = blk_ref[0, b]
    valid = blk_ref[1, b]
    prev = blk_ref[0, jnp.maximum(b - 1, 0)]

    @pl.when((b == 0) | (e != prev))
    def _():
        wg_s[...] = wg_ref[0].astype(_BF16)
        wu_s[...] = wu_ref[0].astype(_BF16)
        wd_s[...] = wd_ref[0].astype(_BF16)

    @pl.when(valid > 0)
    def _():
        keep = lax.broadcasted_iota(jnp.int32, (ROW_BLOCK, 1), 0) < valid
        lo, hi = _unpack_pair(xg_ref[...])
        lo = jnp.where(keep, lo, 0.0).astype(_BF16)
        hi = jnp.where(keep, hi, 0.0).astype(_BF16)
        g = _dot(lo, wg_s[0:half, :]) + _dot(hi, wg_s[half:2 * half, :])
        u = _dot(lo, wu_s[0:half, :]) + _dot(hi, wu_s[half:2 * half, :])
        h = (jax.nn.silu(g) * u).astype(_BF16)
        y_ref[...] = _pack_pair(_dot(h, wd_s[:, 0:half]), _dot(h, wd_s[:, half:2 * half]))

    @pl.when(valid <= 0)
    def _():
        y_ref[...] = jnp.zeros_like(y_ref)


def _expert_ffn(blk, xg, w_gate, w_up, w_down, *, n_blk):
    n_exp, d, hid = w_gate.shape
    half = d // 2
    kern = functools.partial(_ffn_kernel, half=half)
    grid_spec = pltpu.PrefetchScalarGridSpec(
        num_scalar_prefetch=1,
        grid=(n_blk,),
        in_specs=[pl.BlockSpec((ROW_BLOCK, half), lambda b, t: (t[2, b], 0)),
                  pl.BlockSpec((1, d, hid), lambda b, t: (t[0, b], 0, 0)),
                  pl.BlockSpec((1, d, hid), lambda b, t: (t[0, b], 0, 0)),
                  pl.BlockSpec((1, hid, d), lambda b, t: (t[0, b], 0, 0))],
        out_specs=pl.BlockSpec((ROW_BLOCK, half), lambda b, t: (b, 0)),
        scratch_shapes=[pltpu.VMEM((d, hid), _BF16), pltpu.VMEM((d, hid), _BF16), pltpu.VMEM((hid, d), _BF16)],
    )
    return pl.pallas_call(
        kern,
        grid_spec=grid_spec,
        out_shape=jax.ShapeDtypeStruct((n_blk * ROW_BLOCK, half), jnp.uint32),
        compiler_params=_params(),
        name="moe_expert_ffn",
    )(blk, xg, w_gate, w_up, w_down)


def _combine_kernel(x_ref, yg_ref, gate_ref, sg_ref, su_ref, sd_ref, g_ref, b_ref, o_ref, gpad_ref, *,
                    tm, d, alpha):
    x = x_ref[...]
    xb = x.astype(_BF16)
    half = d // 2
    gpad_ref[...] = jnp.zeros_like(gpad_ref)
    gpad_ref[0:TOP_K, :] = gate_ref[...]
    gate_t = gpad_ref[...].T
    lo_acc = jnp.zeros((tm, half), _F32)
    hi_acc = jnp.zeros((tm, half), _F32)
    for k in range(TOP_K):
        lo, hi = _unpack_pair(yg_ref[k])
        w = gate_t[:, k:k + 1]
        lo_acc = lo_acc + w * lo
        hi_acc = hi_acc + w * hi
    hs = (jax.nn.silu(_dot(xb, sg_ref[...])) * _dot(xb, su_ref[...])).astype(_BF16)
    shared = _dot(hs, sd_ref[...])
    f = jnp.concatenate([lo_acc, hi_acc], axis=1) + shared
    o_ref[...] = _layer_norm(alpha * x + f, g_ref[...], b_ref[...])


def _combine(x1, yg, gate, sh_gate, sh_up, sh_down, ln_g, ln_b, *, alpha, tm=256):
    n, d = x1.shape
    tm = min(tm, n)
    hid = sh_gate.shape[1]
    kern = functools.partial(_combine_kernel, tm=tm, d=d, alpha=alpha)
    return pl.pallas_call(
        kern,
        grid=(n // tm,),
        in_specs=[_rows(tm, d), pl.BlockSpec((TOP_K, tm, d // 2), lambda i: (0, i, 0)),
                  pl.BlockSpec((TOP_K, tm), lambda i: (0, i)), _full((d, hid)), _full((d, hid)),
                  _full((hid, d)), _full((1, d)), _full((1, d))],
        out_specs=_rows(tm, d),
        out_shape=jax.ShapeDtypeStruct((n, d), _F32),
        scratch_shapes=[pltpu.VMEM((LANES, tm), _F32)],
        compiler_params=_params(),
        name="moe_combine",
    )(x1, yg, gate, sh_gate.astype(_BF16), sh_up.astype(_BF16), sh_down.astype(_BF16),
      ln_g.reshape(1, d), ln_b.reshape(1, d))


def _dispatch_rows(xp, dest, n_rows):
    n, w = xp.shape
    out = jnp.zeros((n_rows, w), xp.dtype)
    return out.at[dest.reshape(-1)].set(jnp.tile(xp, (dest.shape[0], 1)))


def _gather_rows(yp, dest):
    return yp[dest]


def _moe(x1, x1p, router_w, router_bias, w_gate, w_up, w_down, sh_gate, sh_up, sh_down, ln_g, ln_b, *, alpha):
    n, d = x1.shape
    n_exp = router_w.shape[1]
    n_blk = (n * TOP_K + n_exp * (ROW_BLOCK - 1) + ROW_BLOCK - 1) // ROW_BLOCK
    eidx, gate, rank, cnt = _router(x1, router_w, router_bias)
    dest, blk = _tables(cnt, eidx, rank, n_blk=n_blk)
    xg = _dispatch_rows(x1p, dest, n_blk * ROW_BLOCK)
    y = _expert_ffn(blk, xg, w_gate, w_up, w_down, n_blk=n_blk)
    yg = _gather_rows(y, dest)
    return _combine(x1, yg, gate, sh_gate, sh_up, sh_down, ln_g, ln_b, alpha=alpha)


def kernel(x, a_w_in, a_conv_w, a_w_out, b_w_in, b_v_ln_g, b_v_ln_b, b_w_s, b_s_bias, b_w_out, c_w_in, c_w_grp,
           c_scale, c_w_out, d_w_in, d_conv_w, d_conv_b, d_ln_g, d_ln_b, d_w_out, ln1_g, ln1_b, ln2_g, ln2_b,
           router_w, router_bias, exp_w_gate, exp_w_up, exp_w_down, sh_w_gate, sh_w_up, sh_w_down):
    bsz, seq, d = x.shape
    depth = ln1_g.shape[0]
    n_mixers = 4
    alpha = (2 * depth) ** 0.25
    h = x.reshape(bsz * seq, d)
    for i in range(depth):
        m, j = i % n_mixers, i // n_mixers
        if m == 0:
            x1, x1p = _mixer_a(h, a_w_in[j], a_conv_w[j], a_w_out[j], ln1_g[i], ln1_b[i], seq=seq, alpha=alpha)
        elif m == 1:
            x1, x1p = _mixer_b(h, b_w_in[j], b_v_ln_g[j], b_v_ln_b[j], b_w_s[j], b_s_bias[j], b_w_out[j],
                               ln1_g[i], ln1_b[i], alpha=alpha)
        elif m == 2:
            x1, x1p = _mixer_c(h, c_w_in[j], c_w_grp[j], c_scale[j], c_w_out[j], ln1_g[i], ln1_b[i],
                               seq=seq, alpha=alpha)
        else:
            x1, x1p = _mixer_d(h, d_w_in[j], d_conv_w[j], d_conv_b[j], d_ln_g[j], d_ln_b[j], d_w_out[j],
                               ln1_g[i], ln1_b[i], seq=seq, alpha=alpha)
        h = _moe(x1, x1p, router_w[i], router_bias[i], exp_w_gate[i], exp_w_up[i], exp_w_down[i],
                 sh_w_gate[i], sh_w_up[i], sh_w_down[i], ln2_g[i], ln2_b[i], alpha=alpha)
    return h.reshape(bsz, seq, d)
```

```python
import functools

import jax
import jax.numpy as jnp
from jax import lax
from jax.experimental import pallas as pl
from jax.experimental.pallas import tpu as pltpu
from jax.experimental.pallas import tpu_sc as plsc

LN_EPS = 1e-5
CHUNK = 128
GMLP_HEADS = 8
POOL_WINDOWS = (2, 4, 8, 16)
N_EXPERT_GROUPS = 8
TOPK_GROUPS = 4
TOP_K = 8
ROUTED_SCALE = 2.5
ROW_BLOCK = 512
FFN_SUB = 256
BLK_EXPERT, BLK_VALID, BLK_REGION_END, BLK_N_ACTIVE, BLK_ROWS_USED = 0, 1, 2, 3, 4
LANES = 128
SUBLANES = 8
VMEM_LIMIT = 56 * 1024 * 1024

_BF16 = jnp.bfloat16
_F32 = jnp.float32


def _dot(a, b):
    return jnp.dot(a, b, preferred_element_type=_F32)


def _layer_norm(z, g, b):
    mu = jnp.mean(z, axis=-1, keepdims=True)
    zc = z - mu
    var = jnp.mean(zc * zc, axis=-1, keepdims=True)
    return zc * lax.rsqrt(var + LN_EPS) * g + b


def _pack_pair(lo, hi):
    lo_bits = lax.bitcast_convert_type(lo.astype(_BF16).astype(_F32), jnp.uint32)
    hi_bits = lax.bitcast_convert_type(hi.astype(_BF16).astype(_F32), jnp.uint32)
    return lax.shift_right_logical(lo_bits, jnp.uint32(16)) | (hi_bits & jnp.uint32(0xFFFF0000))


def _unpack_pair(p):
    lo = lax.bitcast_convert_type(lax.shift_left(p, jnp.uint32(16)), _F32)
    hi = lax.bitcast_convert_type(p & jnp.uint32(0xFFFF0000), _F32)
    return lo, hi


def _store_residual_ln(x, y, g_ref, b_ref, alpha, o_ref, op_ref):
    x1 = _layer_norm(alpha * x + y, g_ref[...], b_ref[...])
    o_ref[...] = x1
    half = x1.shape[-1] // 2
    op_ref[...] = _pack_pair(x1[:, :half], x1[:, half:])


def _params(n_axes=1):
    return pltpu.CompilerParams(dimension_semantics=("arbitrary",) * n_axes, vmem_limit_bytes=VMEM_LIMIT)


def _full(shape):
    nd = len(shape)
    return pl.BlockSpec(shape, lambda i, _nd=nd: (0,) * _nd, pipeline_mode=pl.Buffered(1))


def _rows(tm, width):
    return pl.BlockSpec((tm, width), lambda i: (i, 0))


def _mixer_out_shapes(n, d):
    return (jax.ShapeDtypeStruct((n, d), _F32), jax.ShapeDtypeStruct((n, d // 2), jnp.uint32))


def _mixer_a_kernel(x_ref, win_ref, cw_ref, wout_ref, g_ref, b_ref, o_ref, op_ref, zext_ref, *,
                    tm, d, tiles_per_seq, alpha, halo):
    i = pl.program_id(0)
    x = x_ref[...]
    xb = x.astype(_BF16)
    b_gate = _dot(xb, win_ref[:, 0:d])
    z = _dot(xb, win_ref[:, d:2 * d]) * _dot(xb, win_ref[:, 2 * d:3 * d])

    @pl.when(i % tiles_per_seq == 0)
    def _():
        zext_ref[0:halo, :] = jnp.zeros((halo, d), _F32)

    @pl.when(i % tiles_per_seq != 0)
    def _():
        zext_ref[0:halo, :] = zext_ref[tm:tm + halo, :]

    zext_ref[halo:halo + tm, :] = z
    width = cw_ref.shape[0]
    conv = cw_ref[width - 1:width, :] * z
    for k in range(width - 1):
        off = halo - (width - 1) + k
        conv = conv + cw_ref[k:k + 1, :] * zext_ref[off:off + tm, :]
    y = _dot((b_gate * conv).astype(_BF16), wout_ref[...])
    _store_residual_ln(x, y, g_ref, b_ref, alpha, o_ref, op_ref)


def _mixer_a(x2, w_in, conv_w, w_out, ln_g, ln_b, *, seq, alpha, tm=512):
    n, d = x2.shape
    tm = min(tm, seq)
    halo = 8
    kern = functools.partial(_mixer_a_kernel, tm=tm, d=d, tiles_per_seq=seq // tm, alpha=alpha, halo=halo)
    return pl.pallas_call(
        kern,
        grid=(n // tm,),
        in_specs=[_rows(tm, d), _full(w_in.shape), _full(conv_w.shape), _full(w_out.shape),
                  _full((1, d)), _full((1, d))],
        out_specs=(_rows(tm, d), _rows(tm, d // 2)),
        out_shape=_mixer_out_shapes(n, d),
        scratch_shapes=[pltpu.VMEM((tm + halo, d), _F32)],
        compiler_params=_params(),
        name="mixer_a",
    )(x2, w_in.astype(_BF16), conv_w, w_out.astype(_BF16), ln_g.reshape(1, d), ln_b.reshape(1, d))


def _mixer_b_kernel(x_ref, win_ref, vg_ref, vb_ref, ws_ref, sbt_ref, wout_ref, g_ref, b_ref, o_ref, op_ref,
                    gated_ref, *, tm, half, heads, alpha):
    x = x_ref[...]
    xb = x.astype(_BF16)
    u = jax.nn.gelu(_dot(xb, win_ref[:, 0:half]))
    v = jax.nn.gelu(_dot(xb, win_ref[:, half:2 * half]))
    vn = _layer_norm(v, vg_ref[...], vb_ref[...]).astype(_BF16)
    hd = half // heads
    row = lax.broadcasted_iota(jnp.int32, (CHUNK, CHUNK), 0)
    col = lax.broadcasted_iota(jnp.int32, (CHUNK, CHUNK), 1)
    causal = row >= col
    for h in range(heads):
        w_c = jnp.where(causal, ws_ref[h], 0.0).astype(_BF16)
        bias = sbt_ref[:, h:h + 1]
        for c in range(tm // CHUNK):
            rs = slice(c * CHUNK, (c + 1) * CHUNK)
            cs = slice(h * hd, (h + 1) * hd)
            s = _dot(w_c, vn[rs, cs]) + bias
            gated_ref[rs, cs] = (u[rs, cs] * s).astype(_BF16)
    y = _dot(gated_ref[...], wout_ref[...])
    _store_residual_ln(x, y, g_ref, b_ref, alpha, o_ref, op_ref)


def _mixer_b(x2, w_in, v_ln_g, v_ln_b, w_s, s_bias, w_out, ln_g, ln_b, *, alpha, tm=256):
    n, d = x2.shape
    half = w_in.shape[1] // 2
    heads = w_s.shape[0]
    kern = functools.partial(_mixer_b_kernel, tm=tm, half=half, heads=heads, alpha=alpha)
    return pl.pallas_call(
        kern,
        grid=(n // tm,),
        in_specs=[_rows(tm, d), _full(w_in.shape), _full((1, half)), _full((1, half)), _full(w_s.shape),
                  _full((CHUNK, heads)), _full(w_out.shape), _full((1, d)), _full((1, d))],
        out_specs=(_rows(tm, d), _rows(tm, d // 2)),
        out_shape=_mixer_out_shapes(n, d),
        scratch_shapes=[pltpu.VMEM((tm, half), _BF16)],
        compiler_params=_params(),
        name="mixer_b",
    )(x2, w_in.astype(_BF16), v_ln_g.reshape(1, half), v_ln_b.reshape(1, half), w_s, s_bias.T,
      w_out.astype(_BF16), ln_g.reshape(1, d), ln_b.reshape(1, d))


def _mixer_c_kernel(x_ref, win_ref, wgrp_ref, scale_ref, wout_ref, g_ref, b_ref, o_ref, op_ref, hext_ref, *,
                    tm, d, tiles_per_seq, alpha, halo):
    i = pl.program_id(0)
    x = x_ref[...]
    h = _dot(x.astype(_BF16), win_ref[...])

    @pl.when(i % tiles_per_seq == 0)
    def _():
        hext_ref[0:halo, :] = jnp.zeros((halo, d), _F32)

    @pl.when(i % tiles_per_seq != 0)
    def _():
        hext_ref[0:halo, :] = hext_ref[tm:tm + halo, :]

    hext_ref[halo:halo + tm, :] = h
    pos = (i % tiles_per_seq) * tm + lax.broadcasted_iota(jnp.int32, (tm, 1), 0) + 1
    gw = d // len(POOL_WINDOWS)
    y = jnp.zeros((tm, d), _F32)
    for g, win in enumerate(POOL_WINDOWS):
        cs = slice(g * gw, (g + 1) * gw)
        acc = h[:, cs]
        for j in range(1, win):
            acc = acc + hext_ref[halo - j:halo - j + tm, cs]
        mean = acc / jnp.minimum(pos, win).astype(_F32)
        p = mean - h[:, cs]
        yg = _dot(p.astype(_BF16), wgrp_ref[g]) * scale_ref[:, cs]
        y = y + _dot(yg.astype(_BF16), wout_ref[cs, :])
    _store_residual_ln(x, y, g_ref, b_ref, alpha, o_ref, op_ref)


def _mixer_c(x2, w_in, w_grp, scale, w_out, ln_g, ln_b, *, seq, alpha, tm=512):
    n, d = x2.shape
    tm = min(tm, seq)
    halo = 16
    assert max(POOL_WINDOWS) <= halo
    kern = functools.partial(_mixer_c_kernel, tm=tm, d=d, tiles_per_seq=seq // tm, alpha=alpha, halo=halo)
    return pl.pallas_call(
        kern,
        grid=(n // tm,),
        in_specs=[_rows(tm, d), _full(w_in.shape), _full(w_grp.shape), _full((1, d)), _full(w_out.shape),
                  _full((1, d)), _full((1, d))],
        out_specs=(_rows(tm, d), _rows(tm, d // 2)),
        out_shape=_mixer_out_shapes(n, d),
        scratch_shapes=[pltpu.VMEM((tm + halo, d), _F32)],
        compiler_params=_params(),
        name="mixer_c",
    )(x2, w_in.astype(_BF16), w_grp.astype(_BF16), scale.reshape(1, d), w_out.astype(_BF16),
      ln_g.reshape(1, d), ln_b.reshape(1, d))


def _mixer_d_kernel(x_ref, win_ref, cw_ref, cb_ref, cg_ref, cbeta_ref, wout_ref, g_ref, b_ref, o_ref, op_ref,
                    hext_ref, shift_ref, conv_ref, *, tm, d, tiles_per_seq, alpha, halo, row_chunk, lane_chunk):
    i = pl.program_id(0)
    x = x_ref[...]
    xb = x.astype(_BF16)
    h = _dot(xb, win_ref[:, 0:d]) * jax.nn.sigmoid(_dot(xb, win_ref[:, d:2 * d]))

    @pl.when(i % tiles_per_seq == 0)
    def _():
        hext_ref[0:halo, :] = jnp.zeros((halo, d), _F32)

    @pl.when(i % tiles_per_seq != 0)
    def _():
        hext_ref[0:halo, :] = hext_ref[tm:tm + halo, :]

    hext_ref[halo:halo + tm, :] = h
    width = cw_ref.shape[0]
    shift_rows = shift_ref.shape[1]
    for r in range(1, SUBLANES):
        shift_ref[r - 1] = hext_ref[r:r + shift_rows, :]
    for lc in range(d // lane_chunk):
        cs = slice(lc * lane_chunk, (lc + 1) * lane_chunk)
        bias = cb_ref[:, cs]
        for rc in range(tm // row_chunk):
            r0 = rc * row_chunk
            acc = jnp.zeros((row_chunk, lane_chunk), _F32) + bias
            for k in range(width):
                q, r = divmod(halo - (width - 1) + k, SUBLANES)
                off = q * SUBLANES + r0
                if r == 0:
                    src = hext_ref[off:off + row_chunk, cs]
                else:
                    src = shift_ref[r - 1, off:off + row_chunk, cs]
                acc = acc + cw_ref[k:k + 1, cs] * src
            conv_ref[r0:r0 + row_chunk, cs] = acc
    c = _layer_norm(conv_ref[...], cg_ref[...], cbeta_ref[...])
    y = _dot(jax.nn.silu(c).astype(_BF16), wout_ref[...])
    _store_residual_ln(x, y, g_ref, b_ref, alpha, o_ref, op_ref)


def _mixer_d(x2, w_in, conv_w, conv_b, c_ln_g, c_ln_b, w_out, ln_g, ln_b, *, seq, alpha, tm=256):
    n, d = x2.shape
    tm = min(tm, seq)
    halo = 32
    assert conv_w.shape[0] - 1 <= halo
    kern = functools.partial(_mixer_d_kernel, tm=tm, d=d, tiles_per_seq=seq // tm, alpha=alpha, halo=halo,
                             row_chunk=64, lane_chunk=LANES)
    return pl.pallas_call(
        kern,
        grid=(n // tm,),
        in_specs=[_rows(tm, d), _full(w_in.shape), _full(conv_w.shape), _full((1, d)), _full((1, d)),
                  _full((1, d)), _full(w_out.shape), _full((1, d)), _full((1, d))],
        out_specs=(_rows(tm, d), _rows(tm, d // 2)),
        out_shape=_mixer_out_shapes(n, d),
        scratch_shapes=[pltpu.VMEM((tm + halo, d), _F32),
                        pltpu.VMEM((SUBLANES - 1, tm + halo - SUBLANES, d), _F32),
                        pltpu.VMEM((tm, d), _F32)],
        compiler_params=_params(),
        name="mixer_d",
    )(x2, w_in.astype(_BF16), conv_w, conv_b.reshape(1, d), c_ln_g.reshape(1, d), c_ln_b.reshape(1, d),
      w_out.astype(_BF16), ln_g.reshape(1, d), ln_b.reshape(1, d))


def _router_kernel(x_ref, wt_ref, bias_ref, eidx_ref, gate_ref, rank_ref, cnt_ref,
                   carry_ref, gs_ref, masked_ref, graw_ref, *, tm, n_exp):
    i = pl.program_id(0)

    @pl.when(i == 0)
    def _():
        carry_ref[...] = jnp.zeros_like(carry_ref)

    logits = lax.dot_general(wt_ref[...], x_ref[...], (((1,), (1,)), ((), ())),
                             precision=lax.Precision.HIGHEST, preferred_element_type=_F32)
    scores = jax.nn.sigmoid(logits)
    biased = scores + bias_ref[...]
    per = n_exp // N_EXPERT_GROUPS
    neg = -jnp.inf
    sub = lax.broadcasted_iota(jnp.int32, (per, tm), 0)
    for g in range(N_EXPERT_GROUPS):
        bg = biased[g * per:(g + 1) * per, :]
        m1 = jnp.max(bg, axis=0, keepdims=True)
        first = jnp.min(jnp.where(bg == m1, sub, per), axis=0, keepdims=True)
        m2 = jnp.max(jnp.where(sub == first, neg, bg), axis=0, keepdims=True)
        gs_ref[g:g + 1, :] = m1 + m2

    giota = lax.broadcasted_iota(jnp.int32, (N_EXPERT_GROUPS, tm), 0)
    cur = gs_ref[...]
    gsel = jnp.zeros((N_EXPERT_GROUPS, tm), jnp.bool_)
    for _ in range(TOPK_GROUPS):
        m = jnp.max(cur, axis=0, keepdims=True)
        gi = jnp.min(jnp.where(cur == m, giota, N_EXPERT_GROUPS), axis=0, keepdims=True)
        pick = giota == gi
        gsel = gsel | pick
        cur = jnp.where(pick, neg, cur)
    gs_ref[...] = gsel.astype(_F32)
    for g in range(N_EXPERT_GROUPS):
        keep = gs_ref[g:g + 1, :] > 0.5
        masked_ref[g * per:(g + 1) * per, :] = jnp.where(keep, biased[g * per:(g + 1) * per, :], neg)

    eiota = lax.broadcasted_iota(jnp.int32, (n_exp, tm), 0)
    cur = masked_ref[...]
    sel = jnp.zeros((n_exp, tm), jnp.bool_)
    for k in range(TOP_K):
        m = jnp.max(cur, axis=0, keepdims=True)
        ei = jnp.min(jnp.where(cur == m, eiota, n_exp), axis=0, keepdims=True)
        pick = eiota == ei
        eidx_ref[k:k + 1, :] = ei
        graw_ref[k:k + 1, :] = jnp.sum(jnp.where(pick, scores, 0.0), axis=0, keepdims=True)
        sel = sel | pick
        cur = jnp.where(pick, neg, cur)
    graw = graw_ref[...]
    gate_ref[...] = graw / jnp.sum(graw, axis=0, keepdims=True) * ROUTED_SCALE

    sel_b = sel.astype(_F32).astype(_BF16)
    before = (lax.broadcasted_iota(jnp.int32, (tm, tm), 0)
              < lax.broadcasted_iota(jnp.int32, (tm, tm), 1)).astype(_F32).astype(_BF16)
    rank_all = _dot(sel_b, before) + carry_ref[:, 0:1]
    for k in range(TOP_K):
        pick = eiota == eidx_ref[k:k + 1, :]
        rank_ref[k:k + 1, :] = jnp.sum(jnp.where(pick, rank_all, 0.0), axis=0, keepdims=True).astype(jnp.int32)
    carry_ref[...] = carry_ref[...] + jnp.sum(sel.astype(_F32), axis=1, keepdims=True)
    cnt_ref[...] = carry_ref[...]


def _router(x1, router_w, router_bias, *, tm=512):
    n, d = x1.shape
    n_exp = router_w.shape[1]
    tm = min(tm, n)
    kern = functools.partial(_router_kernel, tm=tm, n_exp=n_exp)
    tok = pl.BlockSpec((TOP_K, tm), lambda i: (0, i))
    return pl.pallas_call(
        kern,
        grid=(n // tm,),
        in_specs=[_rows(tm, d), _full((n_exp, d)), _full((n_exp, 1))],
        out_specs=(tok, tok, tok, pl.BlockSpec((n_exp, LANES), lambda i: (0, 0))),
        out_shape=(jax.ShapeDtypeStruct((TOP_K, n), jnp.int32), jax.ShapeDtypeStruct((TOP_K, n), _F32),
                   jax.ShapeDtypeStruct((TOP_K, n), jnp.int32), jax.ShapeDtypeStruct((n_exp, LANES), _F32)),
        scratch_shapes=[pltpu.VMEM((n_exp, LANES), _F32), pltpu.VMEM((N_EXPERT_GROUPS, tm), _F32),
                        pltpu.VMEM((n_exp, tm), _F32), pltpu.VMEM((TOP_K, tm), _F32)],
        compiler_params=_params(),
        name="moe_router",
    )(x1, router_w.T, router_bias.reshape(n_exp, 1))


def _tables_kernel(cnt_ref, eidx_ref, rank_ref, dest_ref, blk_ref, *, n_exp, n_blk_pad):
    counts = cnt_ref[...].astype(jnp.int32)
    shift = ROW_BLOCK.bit_length() - 1
    padded = lax.shift_left(lax.shift_right_logical(counts + (ROW_BLOCK - 1), shift), shift)
    lower = (lax.broadcasted_iota(jnp.int32, (n_exp, n_exp), 1)
             < lax.broadcasted_iota(jnp.int32, (n_exp, n_exp), 0)).astype(_F32)
    pad_start = jnp.dot(lower, padded.astype(_F32), precision=lax.Precision.HIGHEST,
                        preferred_element_type=_F32).astype(jnp.int32)
    pad_end = pad_start + padded
    eidx = eidx_ref[...]
    dest = rank_ref[...]
    for e in range(n_exp):
        dest = dest + jnp.where(eidx == e, pad_start[e:e + 1, 0:1], 0)
    dest_ref[...] = dest

    @pl.when(pl.program_id(0) == 0)
    def _():
        bstart = lax.broadcasted_iota(jnp.int32, (1, n_blk_pad), 1) * ROW_BLOCK
        blk_e = jnp.zeros((1, n_blk_pad), jnp.int32)
        for e in range(n_exp):
            blk_e = blk_e + (pad_end[e:e + 1, 0:1] <= bstart).astype(jnp.int32)
        blk_e = jnp.minimum(blk_e, n_exp - 1)
        used_end = pad_start + counts
        valid = jnp.zeros((1, n_blk_pad), jnp.int32)
        for e in range(n_exp):
            valid = valid + jnp.where(blk_e == e, used_end[e:e + 1, 0:1], 0)
        valid = jnp.clip(valid - bstart, 0, ROW_BLOCK)
        end_blk = lax.shift_right_logical(pad_end, shift)
        region_end = jnp.zeros((1, n_blk_pad), jnp.int32)
        for e in range(n_exp):
            region_end = region_end + jnp.where(blk_e == e, end_blk[e:e + 1, 0:1], 0)
        blk_ref[BLK_EXPERT:BLK_EXPERT + 1, :] = blk_e
        blk_ref[BLK_VALID:BLK_VALID + 1, :] = valid
        blk_ref[BLK_REGION_END:BLK_REGION_END + 1, :] = region_end
        blk_ref[BLK_N_ACTIVE:BLK_N_ACTIVE + 1, :] = jnp.broadcast_to(end_blk[n_exp - 1:n_exp, 0:1], (1, n_blk_pad))
        blk_ref[BLK_ROWS_USED:8, :] = jnp.zeros((8 - BLK_ROWS_USED, n_blk_pad), jnp.int32)


def _tables(cnt, eidx, rank, *, n_blk, tm=2048):
    n = eidx.shape[1]
    n_exp = cnt.shape[0]
    tm = min(tm, n)
    n_blk_pad = pl.cdiv(n_blk, LANES) * LANES
    kern = functools.partial(_tables_kernel, n_exp=n_exp, n_blk_pad=n_blk_pad)
    tok = pl.BlockSpec((TOP_K, tm), lambda i: (0, i))
    return pl.pallas_call(
        kern,
        grid=(n // tm,),
        in_specs=[_full((n_exp, LANES)), tok, tok],
        out_specs=(tok, pl.BlockSpec((8, n_blk_pad), lambda i: (0, 0))),
        out_shape=(jax.ShapeDtypeStruct((TOP_K, n), jnp.int32), jax.ShapeDtypeStruct((8, n_blk_pad), jnp.int32)),
        compiler_params=_params(),
        name="moe_tables",
    )(cnt, eidx, rank)


def _ffn_kernel(blk_ref, xg_hbm, wg_hbm, wu_hbm, wd_hbm, y_hbm,
                xbuf, ybuf, wg_f, wu_f, wd_f, wg_s, wu_s, wd_s, in_sem, out_sem, w_sem, *, half, layer):
    n_active = blk_ref[BLK_N_ACTIVE, 0]

    def in_copy(g, slot):
        return pltpu.make_async_copy(xg_hbm.at[pl.ds(g * ROW_BLOCK, ROW_BLOCK)], xbuf.at[slot], in_sem.at[slot])

    def out_copy(g, slot):
        return pltpu.make_async_copy(ybuf.at[slot], y_hbm.at[pl.ds(g * ROW_BLOCK, ROW_BLOCK)], out_sem.at[slot])

    def weight_copies(e):
        return (pltpu.make_async_copy(wg_hbm.at[layer, e], wg_f, w_sem.at[0]),
                pltpu.make_async_copy(wu_hbm.at[layer, e], wu_f, w_sem.at[1]),
                pltpu.make_async_copy(wd_hbm.at[layer, e], wd_f, w_sem.at[2]))

    def sub_block(slot, s, valid):
        rs = slice(s * FFN_SUB, (s + 1) * FFN_SUB)
        keep = lax.broadcasted_iota(jnp.int32, (FFN_SUB, 1), 0) + s * FFN_SUB < valid
        lo, hi = _unpack_pair(xbuf[slot, rs, :])
        lo = jnp.where(keep, lo, 0.0).astype(_BF16)
        hi = jnp.where(keep, hi, 0.0).astype(_BF16)
        g = _dot(lo, wg_s[0:half, :]) + _dot(hi, wg_s[half:2 * half, :])
        u = _dot(lo, wu_s[0:half, :]) + _dot(hi, wu_s[half:2 * half, :])
        h = (jax.nn.silu(g) * u).astype(_BF16)
        ybuf[slot, rs, :] = _pack_pair(_dot(h, wd_s[:, 0:half]), _dot(h, wd_s[:, half:2 * half]))

    def block(g, slot):
        e = blk_ref[BLK_EXPERT, g]
        valid = blk_ref[BLK_VALID, g]

        @pl.when(g + 1 < n_active)
        def _():
            in_copy(g + 1, 1 - slot).start()

        @pl.when((g == 0) | (e != blk_ref[BLK_EXPERT, jnp.maximum(g - 1, 0)]))
        def _():
            for c in weight_copies(e):
                c.wait()
            wg_s[...] = wg_f[...].astype(_BF16)
            wu_s[...] = wu_f[...].astype(_BF16)
            wd_s[...] = wd_f[...].astype(_BF16)
            nxt = blk_ref[BLK_REGION_END, g]

            @pl.when(nxt < n_active)
            def _():
                for c in weight_copies(blk_ref[BLK_EXPERT, nxt]):
                    c.start()

        in_copy(g, slot).wait()

        @pl.when(g >= 2)
        def _():
            out_copy(g - 2, slot).wait()

        @pl.when(valid > FFN_SUB)
        def _():
            for s in range(ROW_BLOCK // FFN_SUB):
                sub_block(slot, s, valid)

        @pl.when(valid <= FFN_SUB)
        def _():
            sub_block(slot, 0, valid)

        out_copy(g, slot).start()

    in_copy(0, 0).start()
    for c in weight_copies(blk_ref[BLK_EXPERT, 0]):
        c.start()

    def pair(p, carry):
        g = 2 * p
        block(g, 0)

        @pl.when(g + 1 < n_active)
        def _():
            block(g + 1, 1)

        return carry

    lax.fori_loop(0, lax.shift_right_logical(n_active + 1, 1), pair, 0)

    last = n_active - 1
    for slot in range(2):
        @pl.when((last >= slot) & (lax.rem(last - slot, 2) == 0))
        def _():
            out_copy(last, slot).wait()

        @pl.when((last >= 1) & (lax.rem(last - 1 - slot, 2) == 0))
        def _():
            out_copy(last - 1, slot).wait()


def _expert_ffn(blk, xg, w_gate, w_up, w_down, *, n_blk, layer):
    _, n_exp, d, hid = w_gate.shape
    half = d // 2
    kern = functools.partial(_ffn_kernel, half=half, layer=layer)
    any_spec = pl.BlockSpec(memory_space=pl.ANY)
    grid_spec = pltpu.PrefetchScalarGridSpec(
        num_scalar_prefetch=1,
        grid=(1,),
        in_specs=[any_spec, any_spec, any_spec, any_spec],
        out_specs=any_spec,
        scratch_shapes=[pltpu.VMEM((2, ROW_BLOCK, half), jnp.uint32), pltpu.VMEM((2, ROW_BLOCK, half), jnp.uint32),
                        pltpu.VMEM((d, hid), _F32), pltpu.VMEM((d, hid), _F32), pltpu.VMEM((hid, d), _F32),
                        pltpu.VMEM((d, hid), _BF16), pltpu.VMEM((d, hid), _BF16), pltpu.VMEM((hid, d), _BF16),
                        pltpu.SemaphoreType.DMA((2,)), pltpu.SemaphoreType.DMA((2,)), pltpu.SemaphoreType.DMA((3,))],
    )
    return pl.pallas_call(
        kern,
        grid_spec=grid_spec,
        out_shape=jax.ShapeDtypeStruct((n_blk * ROW_BLOCK, half), jnp.uint32),
        compiler_params=_params(),
        name="moe_expert_ffn",
    )(blk, xg, w_gate, w_up, w_down)


def _combine_kernel(x_ref, yg_ref, gate_ref, sg_ref, su_ref, sd_ref, g_ref, b_ref, o_ref, gpad_ref, *,
                    tm, d, alpha):
    x = x_ref[...]
    xb = x.astype(_BF16)
    half = d // 2
    gpad_ref[...] = jnp.zeros_like(gpad_ref)
    gpad_ref[0:TOP_K, :] = gate_ref[...]
    gate_t = gpad_ref[...].T
    lo_acc = jnp.zeros((tm, half), _F32)
    hi_acc = jnp.zeros((tm, half), _F32)
    for k in range(TOP_K):
        lo, hi = _unpack_pair(yg_ref[k])
        w = gate_t[:, k:k + 1]
        lo_acc = lo_acc + w * lo
        hi_acc = hi_acc + w * hi
    hs = (jax.nn.silu(_dot(xb, sg_ref[...])) * _dot(xb, su_ref[...])).astype(_BF16)
    shared = _dot(hs, sd_ref[...])
    f = jnp.concatenate([lo_acc, hi_acc], axis=1) + shared
    o_ref[...] = _layer_norm(alpha * x + f, g_ref[...], b_ref[...])


def _combine(x1, yg, gate, sh_gate, sh_up, sh_down, ln_g, ln_b, *, alpha, tm=256):
    n, d = x1.shape
    tm = min(tm, n)
    hid = sh_gate.shape[1]
    kern = functools.partial(_combine_kernel, tm=tm, d=d, alpha=alpha)
    return pl.pallas_call(
        kern,
        grid=(n // tm,),
        in_specs=[_rows(tm, d), pl.BlockSpec((TOP_K, tm, d // 2), lambda i: (0, i, 0)),
                  pl.BlockSpec((TOP_K, tm), lambda i: (0, i)), _full((d, hid)), _full((d, hid)),
                  _full((hid, d)), _full((1, d)), _full((1, d))],
        out_specs=_rows(tm, d),
        out_shape=jax.ShapeDtypeStruct((n, d), _F32),
        scratch_shapes=[pltpu.VMEM((LANES, tm), _F32)],
        compiler_params=_params(),
        name="moe_combine",
    )(x1, yg, gate, sh_gate.astype(_BF16), sh_up.astype(_BF16), sh_down.astype(_BF16),
      ln_g.reshape(1, d), ln_b.reshape(1, d))


SC_CORES = 2
SC_SUBCORES = 16
SC_WORKERS = SC_CORES * SC_SUBCORES
SC_WINDOW = 64


def _sc_mesh():
    return plsc.VectorSubcoreMesh(core_axis_name="c", subcore_axis_name="s",
                                  num_cores=SC_CORES, num_subcores=SC_SUBCORES)


def _sc_worker_id():
    return lax.axis_index("s") * SC_CORES + lax.axis_index("c")


def _dispatch_rows(xp, dest, n_rows):
    n, width = xp.shape
    n_slots = dest.shape[0]
    n_win = n // SC_WINDOW
    per_worker = n_win // SC_WORKERS
    assert per_worker * SC_WORKERS * SC_WINDOW == n and per_worker % 2 == 0
    dest_w = dest.reshape(n_slots, n_win, SC_WINDOW).transpose(1, 0, 2)

    def body(x_hbm, d_hbm, out_hbm, idx_v, rows_v, row_sem, idx_sem, out_sem):
        first = _sc_worker_id() * per_worker

        def load_rows(j, b):
            return pltpu.make_async_copy(x_hbm.at[pl.ds((first + j) * SC_WINDOW, SC_WINDOW)], rows_v.at[b],
                                         row_sem.at[b])

        def load_idx(j, b):
            return pltpu.make_async_copy(d_hbm.at[first + j], idx_v.at[b], idx_sem.at[b])

        def scatter(b, k):
            return pltpu.make_async_copy(rows_v.at[b], out_hbm.at[idx_v.at[b].at[k]], out_sem.at[b])

        for b in range(2):
            load_rows(b, b).start()
            load_idx(b, b).start()

        @pl.loop(0, per_worker, step=2)
        def _(j0):
            for b in range(2):
                j = j0 + b
                load_rows(j, b).wait()
                load_idx(j, b).wait()
                for k in range(n_slots):
                    scatter(b, k).start()
                for k in range(n_slots):
                    scatter(b, k).wait()

                @pl.when(j + 2 < per_worker)
                def _():
                    load_rows(j + 2, b).start()
                    load_idx(j + 2, b).start()

    return pl.kernel(
        body,
        out_type=jax.ShapeDtypeStruct((n_rows, width), xp.dtype),
        mesh=_sc_mesh(),
        scratch_types=[pltpu.VMEM((2, n_slots, SC_WINDOW), jnp.int32), pltpu.VMEM((2, SC_WINDOW, width), xp.dtype),
                       pltpu.SemaphoreType.DMA((2,)), pltpu.SemaphoreType.DMA((2,)), pltpu.SemaphoreType.DMA((2,))],
        name="moe_dispatch_rows",
    )(xp, dest_w)


def _gather_rows(yp, dest):
    n_slots, n = dest.shape
    width = yp.shape[1]
    m = n_slots * n
    per_worker = m // SC_WORKERS
    n_win = per_worker // SC_WINDOW
    assert n_win * SC_WINDOW * SC_WORKERS == m and n_win % 2 == 0

    def body(y_hbm, idx_hbm, out_hbm, idx_v, rows_v, in_sem, out_sem):
        base = _sc_worker_id() * per_worker
        pltpu.sync_copy(idx_hbm.at[pl.ds(base, per_worker)], idx_v)

        def gather(w, b):
            return pltpu.make_async_copy(y_hbm.at[idx_v.at[pl.ds(w * SC_WINDOW, SC_WINDOW)]], rows_v.at[b],
                                         in_sem.at[b])

        def put(w, b):
            return pltpu.make_async_copy(rows_v.at[b], out_hbm.at[pl.ds(base + w * SC_WINDOW, SC_WINDOW)],
                                         out_sem.at[b])

        for b in range(2):
            gather(b, b).start()

        @pl.loop(0, n_win, step=2)
        def _(w0):
            for b in range(2):
                w = w0 + b
                gather(w, b).wait()
                put(w, b).start()

                @pl.when(w >= 1)
                def _():
                    put(w - 1, 1 - b).wait()

                    @pl.when(w + 1 < n_win)
                    def _():
                        gather(w + 1, 1 - b).start()

        put(n_win - 1, 1).wait()

    out = pl.kernel(
        body,
        out_type=jax.ShapeDtypeStruct((m, width), yp.dtype),
        mesh=_sc_mesh(),
        scratch_types=[pltpu.VMEM((per_worker,), jnp.int32), pltpu.VMEM((2, SC_WINDOW, width), yp.dtype),
                       pltpu.SemaphoreType.DMA((2,)), pltpu.SemaphoreType.DMA((2,))],
        name="moe_gather_rows",
    )(yp, dest.reshape(m))
    return out.reshape(n_slots, n, width)


def _moe(x1, x1p, router_w, router_bias, w_gate, w_up, w_down, sh_gate, sh_up, sh_down, ln_g, ln_b, *, alpha,
         layer):
    n, d = x1.shape
    n_exp = router_w.shape[1]
    n_blk = (n * TOP_K + n_exp * (ROW_BLOCK - 1) + ROW_BLOCK - 1) // ROW_BLOCK
    eidx, gate, rank, cnt = _router(x1, router_w, router_bias)
    dest, blk = _tables(cnt, eidx, rank, n_blk=n_blk)
    xg = _dispatch_rows(x1p, dest, n_blk * ROW_BLOCK)
    y = _expert_ffn(blk, xg, w_gate, w_up, w_down, n_blk=n_blk, layer=layer)
    yg = _gather_rows(y, dest)
    return _combine(x1, yg, gate, sh_gate, sh_up, sh_down, ln_g, ln_b, alpha=alpha)


def kernel(x, a_w_in, a_conv_w, a_w_out, b_w_in, b_v_ln_g, b_v_ln_b, b_w_s, b_s_bias, b_w_out, c_w_in, c_w_grp,
           c_scale, c_w_out, d_w_in, d_conv_w, d_conv_b, d_ln_g, d_ln_b, d_w_out, ln1_g, ln1_b, ln2_g, ln2_b,
           router_w, router_bias, exp_w_gate, exp_w_up, exp_w_down, sh_w_gate, sh_w_up, sh_w_down):
    bsz, seq, d = x.shape
    depth = ln1_g.shape[0]
    n_mixers = 4
    alpha = (2 * depth) ** 0.25
    h = x.reshape(bsz * seq, d)
    for i in range(depth):
        m, j = i % n_mixers, i // n_mixers
        if m == 0:
            x1, x1p = _mixer_a(h, a_w_in[j], a_conv_w[j], a_w_out[j], ln1_g[i], ln1_b[i], seq=seq, alpha=alpha)
        elif m == 1:
            x1, x1p = _mixer_b(h, b_w_in[j], b_v_ln_g[j], b_v_ln_b[j], b_w_s[j], b_s_bias[j], b_w_out[j],
                               ln1_g[i], ln1_b[i], alpha=alpha)
        elif m == 2:
            x1, x1p = _mixer_c(h, c_w_in[j], c_w_grp[j], c_scale[j], c_w_out[j], ln1_g[i], ln1_b[i],
                               seq=seq, alpha=alpha)
        else:
            x1, x1p = _mixer_d(h, d_w_in[j], d_conv_w[j], d_conv_b[j], d_ln_g[j], d_ln_b[j], d_w_out[j],
                               ln1_g[i], ln1_b[i], seq=seq, alpha=alpha)
        h = _moe(x1, x1p, router_w[i], router_bias[i], exp_w_gate, exp_w_up, exp_w_down,
                 sh_w_gate[i], sh_w_up[i], sh_w_down[i], ln2_g[i], ln2_b[i], alpha=alpha, layer=i)
    return h.reshape(bsz, seq, d)
```

```python
import functools

import jax
import jax.numpy as jnp
from jax import lax
from jax.experimental import pallas as pl
from jax.experimental.pallas import tpu as pltpu
from jax.experimental.pallas import tpu_sc as plsc

LN_EPS = 1e-5
CHUNK = 128
GMLP_HEADS = 8
POOL_WINDOWS = (2, 4, 8, 16)
N_EXPERT_GROUPS = 8
TOPK_GROUPS = 4
TOP_K = 8
ROUTED_SCALE = 2.5
ROW_BLOCK = 512
FFN_SUB = 256
BLK_EXPERT, BLK_VALID, BLK_REGION_END, BLK_N_ACTIVE, BLK_ROWS_USED = 0, 1, 2, 3, 4
LANES = 128
SUBLANES = 8
VMEM_LIMIT = 56 * 1024 * 1024

_BF16 = jnp.bfloat16
_F32 = jnp.float32


def _dot(a, b):
    return jnp.dot(a, b, preferred_element_type=_F32)


def _layer_norm(z, g, b):
    mu = jnp.mean(z, axis=-1, keepdims=True)
    zc = z - mu
    var = jnp.mean(zc * zc, axis=-1, keepdims=True)
    return zc * lax.rsqrt(var + LN_EPS) * g + b


def _pack_pair(lo, hi):
    lo_bits = lax.bitcast_convert_type(lo.astype(_BF16).astype(_F32), jnp.uint32)
    hi_bits = lax.bitcast_convert_type(hi.astype(_BF16).astype(_F32), jnp.uint32)
    return lax.shift_right_logical(lo_bits, jnp.uint32(16)) | (hi_bits & jnp.uint32(0xFFFF0000))


def _unpack_pair(p):
    lo = lax.bitcast_convert_type(lax.shift_left(p, jnp.uint32(16)), _F32)
    hi = lax.bitcast_convert_type(p & jnp.uint32(0xFFFF0000), _F32)
    return lo, hi


def _store_residual_ln(x, y, g_ref, b_ref, alpha, o_ref, op_ref):
    x1 = _layer_norm(alpha * x + y, g_ref[...], b_ref[...])
    o_ref[...] = x1
    half = x1.shape[-1] // 2
    op_ref[...] = _pack_pair(x1[:, :half], x1[:, half:])


def _params(n_axes=1):
    return pltpu.CompilerParams(dimension_semantics=("arbitrary",) * n_axes, vmem_limit_bytes=VMEM_LIMIT)


def _full(shape):
    nd = len(shape)
    return pl.BlockSpec(shape, lambda i, _nd=nd: (0,) * _nd, pipeline_mode=pl.Buffered(1))


def _rows(tm, width):
    return pl.BlockSpec((tm, width), lambda i: (i, 0))


def _mixer_out_shapes(n, d):
    return (jax.ShapeDtypeStruct((n, d), _F32), jax.ShapeDtypeStruct((n, d // 2), jnp.uint32))


def _mixer_a_kernel(x_ref, win_ref, cw_ref, wout_ref, g_ref, b_ref, o_ref, op_ref, zext_ref, *,
                    tm, d, tiles_per_seq, alpha, halo):
    i = pl.program_id(0)
    x = x_ref[...]
    xb = x.astype(_BF16)
    b_gate = _dot(xb, win_ref[:, 0:d])
    z = _dot(xb, win_ref[:, d:2 * d]) * _dot(xb, win_ref[:, 2 * d:3 * d])

    @pl.when(i % tiles_per_seq == 0)
    def _():
        zext_ref[0:halo, :] = jnp.zeros((halo, d), _F32)

    @pl.when(i % tiles_per_seq != 0)
    def _():
        zext_ref[0:halo, :] = zext_ref[tm:tm + halo, :]

    zext_ref[halo:halo + tm, :] = z
    width = cw_ref.shape[0]
    conv = cw_ref[width - 1:width, :] * z
    for k in range(width - 1):
        off = halo - (width - 1) + k
        conv = conv + cw_ref[k:k + 1, :] * zext_ref[off:off + tm, :]
    y = _dot((b_gate * conv).astype(_BF16), wout_ref[...])
    _store_residual_ln(x, y, g_ref, b_ref, alpha, o_ref, op_ref)


def _mixer_a(x2, w_in, conv_w, w_out, ln_g, ln_b, *, seq, alpha, tm=512):
    n, d = x2.shape
    tm = min(tm, seq)
    halo = 8
    kern = functools.partial(_mixer_a_kernel, tm=tm, d=d, tiles_per_seq=seq // tm, alpha=alpha, halo=halo)
    return pl.pallas_call(
        kern,
        grid=(n // tm,),
        in_specs=[_rows(tm, d), _full(w_in.shape), _full(conv_w.shape), _full(w_out.shape),
                  _full((1, d)), _full((1, d))],
        out_specs=(_rows(tm, d), _rows(tm, d // 2)),
        out_shape=_mixer_out_shapes(n, d),
        scratch_shapes=[pltpu.VMEM((tm + halo, d), _F32)],
        compiler_params=_params(),
        name="mixer_a",
    )(x2, w_in.astype(_BF16), conv_w, w_out.astype(_BF16), ln_g.reshape(1, d), ln_b.reshape(1, d))


def _mixer_b_kernel(x_ref, win_ref, vg_ref, vb_ref, ws_ref, sbt_ref, wout_ref, g_ref, b_ref, o_ref, op_ref,
                    gated_ref, *, tm, d, half, heads, alpha, row_group):
    groups = [slice(r, r + row_group) for r in range(0, tm, row_group)]
    xs, us, vns = [], [], []
    for rows in groups:
        x = x_ref[rows, :]
        xb = x.astype(_BF16)
        u = jax.nn.gelu(_dot(xb, win_ref[:, 0:half]))
        v = jax.nn.gelu(_dot(xb, win_ref[:, half:2 * half]))
        xs.append(x)
        us.append(u)
        vns.append(_layer_norm(v, vg_ref[...], vb_ref[...]).astype(_BF16))
    hd = half // heads
    row = lax.broadcasted_iota(jnp.int32, (CHUNK, CHUNK), 0)
    col = lax.broadcasted_iota(jnp.int32, (CHUNK, CHUNK), 1)
    causal = row >= col
    for h in range(heads):
        w_c = jnp.where(causal, ws_ref[h], 0.0).astype(_BF16)
        bias = sbt_ref[:, h:h + 1]
        cs = slice(h * hd, (h + 1) * hd)
        for rows, u, vn in zip(groups, us, vns):
            for c in range(row_group // CHUNK):
                rs = slice(c * CHUNK, (c + 1) * CHUNK)
                s = _dot(w_c, vn[rs, cs]) + bias
                gated_ref[rows.start + c * CHUNK:rows.start + (c + 1) * CHUNK, cs] = (u[rs, cs] * s).astype(_BF16)
    for rows, x in zip(groups, xs):
        y = _dot(gated_ref[rows, :], wout_ref[...])
        x1 = _layer_norm(alpha * x + y, g_ref[...], b_ref[...])
        o_ref[rows, :] = x1
        op_ref[rows, :] = _pack_pair(x1[:, :d // 2], x1[:, d // 2:])


def _mixer_b(x2, w_in, v_ln_g, v_ln_b, w_s, s_bias, w_out, ln_g, ln_b, *, alpha, tm=512, row_group=256):
    n, d = x2.shape
    half = w_in.shape[1] // 2
    heads = w_s.shape[0]
    kern = functools.partial(_mixer_b_kernel, tm=tm, d=d, half=half, heads=heads, alpha=alpha,
                             row_group=row_group)
    return pl.pallas_call(
        kern,
        grid=(n // tm,),
        in_specs=[_rows(tm, d), _full(w_in.shape), _full((1, half)), _full((1, half)), _full(w_s.shape),
                  _full((CHUNK, heads)), _full(w_out.shape), _full((1, d)), _full((1, d))],
        out_specs=(_rows(tm, d), _rows(tm, d // 2)),
        out_shape=_mixer_out_shapes(n, d),
        scratch_shapes=[pltpu.VMEM((tm, half), _BF16)],
        compiler_params=_params(),
        name="mixer_b",
    )(x2, w_in.astype(_BF16), v_ln_g.reshape(1, half), v_ln_b.reshape(1, half), w_s, s_bias.T,
      w_out.astype(_BF16), ln_g.reshape(1, d), ln_b.reshape(1, d))


def _mixer_c_kernel(x_ref, win_ref, wgrp_ref, scale_ref, wout_ref, g_ref, b_ref, o_ref, op_ref, hext_ref, *,
                    tm, d, tiles_per_seq, alpha, halo):
    i = pl.program_id(0)
    x = x_ref[...]
    h = _dot(x.astype(_BF16), win_ref[...])

    @pl.when(i % tiles_per_seq == 0)
    def _():
        hext_ref[0:halo, :] = jnp.zeros((halo, d), _F32)

    @pl.when(i % tiles_per_seq != 0)
    def _():
        hext_ref[0:halo, :] = hext_ref[tm:tm + halo, :]

    hext_ref[halo:halo + tm, :] = h
    pos = (i % tiles_per_seq) * tm + lax.broadcasted_iota(jnp.int32, (tm, 1), 0) + 1
    gw = d // len(POOL_WINDOWS)
    y = jnp.zeros((tm, d), _F32)
    for g, win in enumerate(POOL_WINDOWS):
        cs = slice(g * gw, (g + 1) * gw)
        acc = h[:, cs]
        for j in range(1, win):
            acc = acc + hext_ref[halo - j:halo - j + tm, cs]
        mean = acc / jnp.minimum(pos, win).astype(_F32)
        p = mean - h[:, cs]
        yg = _dot(p.astype(_BF16), wgrp_ref[g]) * scale_ref[:, cs]
        y = y + _dot(yg.astype(_BF16), wout_ref[cs, :])
    _store_residual_ln(x, y, g_ref, b_ref, alpha, o_ref, op_ref)


def _mixer_c(x2, w_in, w_grp, scale, w_out, ln_g, ln_b, *, seq, alpha, tm=512):
    n, d = x2.shape
    tm = min(tm, seq)
    halo = 16
    assert max(POOL_WINDOWS) <= halo
    kern = functools.partial(_mixer_c_kernel, tm=tm, d=d, tiles_per_seq=seq // tm, alpha=alpha, halo=halo)
    return pl.pallas_call(
        kern,
        grid=(n // tm,),
        in_specs=[_rows(tm, d), _full(w_in.shape), _full(w_grp.shape), _full((1, d)), _full(w_out.shape),
                  _full((1, d)), _full((1, d))],
        out_specs=(_rows(tm, d), _rows(tm, d // 2)),
        out_shape=_mixer_out_shapes(n, d),
        scratch_shapes=[pltpu.VMEM((tm + halo, d), _F32)],
        compiler_params=_params(),
        name="mixer_c",
    )(x2, w_in.astype(_BF16), w_grp.astype(_BF16), scale.reshape(1, d), w_out.astype(_BF16),
      ln_g.reshape(1, d), ln_b.reshape(1, d))


def _mixer_d_kernel(x_ref, win_ref, cw_ref, cb_ref, cg_ref, cbeta_ref, wout_ref, g_ref, b_ref, o_ref, op_ref,
                    hext_ref, shift_ref, conv_ref, *, tm, d, tiles_per_seq, alpha, halo, row_chunk, lane_chunk,
                    row_group):
    i = pl.program_id(0)

    @pl.when(i % tiles_per_seq == 0)
    def _():
        hext_ref[0:halo, :] = jnp.zeros((halo, d), _F32)

    @pl.when(i % tiles_per_seq != 0)
    def _():
        hext_ref[0:halo, :] = hext_ref[tm:tm + halo, :]

    groups = range(0, tm, row_group)
    for g0 in groups:
        xb = x_ref[g0:g0 + row_group, :].astype(_BF16)
        hext_ref[halo + g0:halo + g0 + row_group, :] = (
            _dot(xb, win_ref[:, 0:d]) * jax.nn.sigmoid(_dot(xb, win_ref[:, d:2 * d])))
    width = cw_ref.shape[0]
    shift_rows = row_group + halo - SUBLANES
    for g0 in groups:
        for r in range(1, SUBLANES):
            shift_ref[r - 1, g0:g0 + shift_rows, :] = hext_ref[g0 + r:g0 + r + shift_rows, :]
        for lc in range(d // lane_chunk):
            cs = slice(lc * lane_chunk, (lc + 1) * lane_chunk)
            bias = cb_ref[:, cs]
            for r0 in range(g0, g0 + row_group, row_chunk):
                acc = jnp.zeros((row_chunk, lane_chunk), _F32) + bias
                for k in range(width):
                    q, r = divmod(halo - (width - 1) + k, SUBLANES)
                    off = q * SUBLANES + r0
                    if r == 0:
                        src = hext_ref[off:off + row_chunk, cs]
                    else:
                        src = shift_ref[r - 1, off:off + row_chunk, cs]
                    acc = acc + cw_ref[k:k + 1, cs] * src
                conv_ref[r0:r0 + row_chunk, cs] = acc
    for g0 in groups:
        rows = slice(g0, g0 + row_group)
        c = _layer_norm(conv_ref[rows, :], cg_ref[...], cbeta_ref[...])
        y = _dot(jax.nn.silu(c).astype(_BF16), wout_ref[...])
        x1 = _layer_norm(alpha * x_ref[rows, :] + y, g_ref[...], b_ref[...])
        o_ref[rows, :] = x1
        op_ref[rows, :] = _pack_pair(x1[:, :d // 2], x1[:, d // 2:])


def _mixer_d(x2, w_in, conv_w, conv_b, c_ln_g, c_ln_b, w_out, ln_g, ln_b, *, seq, alpha, tm=512, row_group=256):
    n, d = x2.shape
    tm = min(tm, seq)
    row_group = min(row_group, tm)
    halo = 32
    assert conv_w.shape[0] - 1 <= halo
    kern = functools.partial(_mixer_d_kernel, tm=tm, d=d, tiles_per_seq=seq // tm, alpha=alpha, halo=halo,
                             row_chunk=64, lane_chunk=LANES, row_group=row_group)
    return pl.pallas_call(
        kern,
        grid=(n // tm,),
        in_specs=[_rows(tm, d), _full(w_in.shape), _full(conv_w.shape), _full((1, d)), _full((1, d)),
                  _full((1, d)), _full(w_out.shape), _full((1, d)), _full((1, d))],
        out_specs=(_rows(tm, d), _rows(tm, d // 2)),
        out_shape=_mixer_out_shapes(n, d),
        scratch_shapes=[pltpu.VMEM((tm + halo, d), _F32),
                        pltpu.VMEM((SUBLANES - 1, tm + halo - SUBLANES, d), _F32),
                        pltpu.VMEM((tm, d), _F32)],
        compiler_params=_params(),
        name="mixer_d",
    )(x2, w_in.astype(_BF16), conv_w, conv_b.reshape(1, d), c_ln_g.reshape(1, d), c_ln_b.reshape(1, d),
      w_out.astype(_BF16), ln_g.reshape(1, d), ln_b.reshape(1, d))


def _router_kernel(x_ref, wt_ref, bias_ref, eidx_ref, gate_ref, rank_ref, cnt_ref,
                   carry_ref, gs_ref, masked_ref, graw_ref, *, tm, n_exp):
    i = pl.program_id(0)

    @pl.when(i == 0)
    def _():
        carry_ref[...] = jnp.zeros_like(carry_ref)

    def split(v):
        hi = v.astype(_BF16)
        return hi, (v - hi.astype(_F32)).astype(_BF16)

    def dot_nt(a, b):
        return lax.dot_general(a, b, (((1,), (1,)), ((), ())), preferred_element_type=_F32)

    w_hi, w_lo = split(wt_ref[...])
    x_hi, x_lo = split(x_ref[...])
    logits = dot_nt(w_hi, x_hi) + (dot_nt(w_hi, x_lo) + dot_nt(w_lo, x_hi))
    scores = jax.nn.sigmoid(logits)
    biased = scores + bias_ref[...]
    per = n_exp // N_EXPERT_GROUPS
    neg = -jnp.inf
    sub = lax.broadcasted_iota(jnp.int32, (per, tm), 0)
    for g in range(N_EXPERT_GROUPS):
        bg = biased[g * per:(g + 1) * per, :]
        m1 = jnp.max(bg, axis=0, keepdims=True)
        first = jnp.min(jnp.where(bg == m1, sub, per), axis=0, keepdims=True)
        m2 = jnp.max(jnp.where(sub == first, neg, bg), axis=0, keepdims=True)
        gs_ref[g:g + 1, :] = m1 + m2

    giota = lax.broadcasted_iota(jnp.int32, (N_EXPERT_GROUPS, tm), 0)
    cur = gs_ref[...]
    gsel = jnp.zeros((N_EXPERT_GROUPS, tm), jnp.bool_)
    for _ in range(TOPK_GROUPS):
        m = jnp.max(cur, axis=0, keepdims=True)
        gi = jnp.min(jnp.where(cur == m, giota, N_EXPERT_GROUPS), axis=0, keepdims=True)
        pick = giota == gi
        gsel = gsel | pick
        cur = jnp.where(pick, neg, cur)
    gs_ref[...] = gsel.astype(_F32)
    for g in range(N_EXPERT_GROUPS):
        keep = gs_ref[g:g + 1, :] > 0.5
        masked_ref[g * per:(g + 1) * per, :] = jnp.where(keep, biased[g * per:(g + 1) * per, :], neg)

    eiota = lax.broadcasted_iota(jnp.int32, (n_exp, tm), 0)
    cur = masked_ref[...]
    sel = jnp.zeros((n_exp, tm), jnp.bool_)
    for k in range(TOP_K):
        m = jnp.max(cur, axis=0, keepdims=True)
        ei = jnp.min(jnp.where(cur == m, eiota, n_exp), axis=0, keepdims=True)
        pick = eiota == ei
        eidx_ref[k:k + 1, :] = ei
        graw_ref[k:k + 1, :] = jnp.sum(jnp.where(pick, scores, 0.0), axis=0, keepdims=True)
        sel = sel | pick
        cur = jnp.where(pick, neg, cur)
    graw = graw_ref[...]
    gate_ref[...] = graw / jnp.sum(graw, axis=0, keepdims=True) * ROUTED_SCALE

    sel_b = sel.astype(_F32).astype(_BF16)
    before = (lax.broadcasted_iota(jnp.int32, (tm, tm), 0)
              < lax.broadcasted_iota(jnp.int32, (tm, tm), 1)).astype(_F32).astype(_BF16)
    rank_all = _dot(sel_b, before) + carry_ref[:, 0:1]
    for k in range(TOP_K):
        pick = eiota == eidx_ref[k:k + 1, :]
        rank_ref[k:k + 1, :] = jnp.sum(jnp.where(pick, rank_all, 0.0), axis=0, keepdims=True).astype(jnp.int32)
    carry_ref[...] = carry_ref[...] + jnp.sum(sel.astype(_F32), axis=1, keepdims=True)
    cnt_ref[...] = carry_ref[...]


def _router(x1, router_w, router_bias, *, tm=512):
    n, d = x1.shape
    n_exp = router_w.shape[1]
    tm = min(tm, n)
    kern = functools.partial(_router_kernel, tm=tm, n_exp=n_exp)
    tok = pl.BlockSpec((TOP_K, tm), lambda i: (0, i))
    return pl.pallas_call(
        kern,
        grid=(n // tm,),
        in_specs=[_rows(tm, d), _full((n_exp, d)), _full((n_exp, 1))],
        out_specs=(tok, tok, tok, pl.BlockSpec((n_exp, LANES), lambda i: (0, 0))),
        out_shape=(jax.ShapeDtypeStruct((TOP_K, n), jnp.int32), jax.ShapeDtypeStruct((TOP_K, n), _F32),
                   jax.ShapeDtypeStruct((TOP_K, n), jnp.int32), jax.ShapeDtypeStruct((n_exp, LANES), _F32)),
        scratch_shapes=[pltpu.VMEM((n_exp, LANES), _F32), pltpu.VMEM((N_EXPERT_GROUPS, tm), _F32),
                        pltpu.VMEM((n_exp, tm), _F32), pltpu.VMEM((TOP_K, tm), _F32)],
        compiler_params=_params(),
        name="moe_router",
    )(x1, router_w.T, router_bias.reshape(n_exp, 1))


def _tables_kernel(cnt_ref, eidx_ref, rank_ref, dest_ref, blk_ref, *, n_exp, n_blk_pad):
    counts = cnt_ref[...].astype(jnp.int32)
    shift = ROW_BLOCK.bit_length() - 1
    padded = lax.shift_left(lax.shift_right_logical(counts + (ROW_BLOCK - 1), shift), shift)
    lower = (lax.broadcasted_iota(jnp.int32, (n_exp, n_exp), 1)
             < lax.broadcasted_iota(jnp.int32, (n_exp, n_exp), 0)).astype(_F32)
    pad_start = jnp.dot(lower, padded.astype(_F32), precision=lax.Precision.HIGHEST,
                        preferred_element_type=_F32).astype(jnp.int32)
    pad_end = pad_start + padded
    eidx = eidx_ref[...]
    dest = rank_ref[...]
    for e in range(n_exp):
        dest = dest + jnp.where(eidx == e, pad_start[e:e + 1, 0:1], 0)
    dest_ref[...] = dest

    @pl.when(pl.program_id(0) == 0)
    def _():
        bstart = lax.broadcasted_iota(jnp.int32, (1, n_blk_pad), 1) * ROW_BLOCK
        blk_e = jnp.zeros((1, n_blk_pad), jnp.int32)
        for e in range(n_exp):
            blk_e = blk_e + (pad_end[e:e + 1, 0:1] <= bstart).astype(jnp.int32)
        blk_e = jnp.minimum(blk_e, n_exp - 1)
        used_end = pad_start + counts
        valid = jnp.zeros((1, n_blk_pad), jnp.int32)
        for e in range(n_exp):
            valid = valid + jnp.where(blk_e == e, used_end[e:e + 1, 0:1], 0)
        valid = jnp.clip(valid - bstart, 0, ROW_BLOCK)
        end_blk = lax.shift_right_logical(pad_end, shift)
        region_end = jnp.zeros((1, n_blk_pad), jnp.int32)
        for e in range(n_exp):
            region_end = region_end + jnp.where(blk_e == e, end_blk[e:e + 1, 0:1], 0)
        blk_ref[BLK_EXPERT:BLK_EXPERT + 1, :] = blk_e
        blk_ref[BLK_VALID:BLK_VALID + 1, :] = valid
        blk_ref[BLK_REGION_END:BLK_REGION_END + 1, :] = region_end
        blk_ref[BLK_N_ACTIVE:BLK_N_ACTIVE + 1, :] = jnp.broadcast_to(end_blk[n_exp - 1:n_exp, 0:1], (1, n_blk_pad))
        blk_ref[BLK_ROWS_USED:8, :] = jnp.zeros((8 - BLK_ROWS_USED, n_blk_pad), jnp.int32)


def _tables(cnt, eidx, rank, *, n_blk, tm=2048):
    n = eidx.shape[1]
    n_exp = cnt.shape[0]
    tm = min(tm, n)
    n_blk_pad = pl.cdiv(n_blk, LANES) * LANES
    kern = functools.partial(_tables_kernel, n_exp=n_exp, n_blk_pad=n_blk_pad)
    tok = pl.BlockSpec((TOP_K, tm), lambda i: (0, i))
    return pl.pallas_call(
        kern,
        grid=(n // tm,),
        in_specs=[_full((n_exp, LANES)), tok, tok],
        out_specs=(tok, pl.BlockSpec((8, n_blk_pad), lambda i: (0, 0))),
        out_shape=(jax.ShapeDtypeStruct((TOP_K, n), jnp.int32), jax.ShapeDtypeStruct((8, n_blk_pad), jnp.int32)),
        compiler_params=_params(),
        name="moe_tables",
    )(cnt, eidx, rank)


def _ffn_kernel(blk_ref, xg_hbm, wg_hbm, wu_hbm, wd_hbm, y_hbm,
                xbuf, ybuf, wg_f, wu_f, wd_f, wg_s, wu_s, wd_s, in_sem, out_sem, w_sem, *, half, layer):
    n_active = blk_ref[BLK_N_ACTIVE, 0]

    def in_copy(g, slot):
        return pltpu.make_async_copy(xg_hbm.at[pl.ds(g * ROW_BLOCK, ROW_BLOCK)], xbuf.at[slot], in_sem.at[slot])

    def out_copy(g, slot):
        return pltpu.make_async_copy(ybuf.at[slot], y_hbm.at[pl.ds(g * ROW_BLOCK, ROW_BLOCK)], out_sem.at[slot])

    def weight_copies(e):
        return (pltpu.make_async_copy(wg_hbm.at[layer, e], wg_f, w_sem.at[0]),
                pltpu.make_async_copy(wu_hbm.at[layer, e], wu_f, w_sem.at[1]),
                pltpu.make_async_copy(wd_hbm.at[layer, e], wd_f, w_sem.at[2]))

    def sub_blocks(slot, n_sub, valid):
        hidden = []
        for s in range(n_sub):
            rs = slice(s * FFN_SUB, (s + 1) * FFN_SUB)
            keep = lax.broadcasted_iota(jnp.int32, (FFN_SUB, 1), 0) + s * FFN_SUB < valid
            lo, hi = _unpack_pair(xbuf[slot, rs, :])
            lo = jnp.where(keep, lo, 0.0).astype(_BF16)
            hi = jnp.where(keep, hi, 0.0).astype(_BF16)
            g = _dot(lo, wg_s[0:half, :]) + _dot(hi, wg_s[half:2 * half, :])
            u = _dot(lo, wu_s[0:half, :]) + _dot(hi, wu_s[half:2 * half, :])
            hidden.append((jax.nn.silu(g) * u).astype(_BF16))
        for s, h in enumerate(hidden):
            rs = slice(s * FFN_SUB, (s + 1) * FFN_SUB)
            ybuf[slot, rs, :] = _pack_pair(_dot(h, wd_s[:, 0:half]), _dot(h, wd_s[:, half:2 * half]))

    def block(g, slot):
        e = blk_ref[BLK_EXPERT, g]
        valid = blk_ref[BLK_VALID, g]

        @pl.when(g + 1 < n_active)
        def _():
            in_copy(g + 1, 1 - slot).start()

        @pl.when((g == 0) | (e != blk_ref[BLK_EXPERT, jnp.maximum(g - 1, 0)]))
        def _():
            for c in weight_copies(e):
                c.wait()
            wg_s[...] = wg_f[...].astype(_BF16)
            wu_s[...] = wu_f[...].astype(_BF16)
            wd_s[...] = wd_f[...].astype(_BF16)
            nxt = blk_ref[BLK_REGION_END, g]

            @pl.when(nxt < n_active)
            def _():
                for c in weight_copies(blk_ref[BLK_EXPERT, nxt]):
                    c.start()

        in_copy(g, slot).wait()

        @pl.when(g >= 2)
        def _():
            out_copy(g - 2, slot).wait()

        @pl.when(valid > FFN_SUB)
        def _():
            sub_blocks(slot, ROW_BLOCK // FFN_SUB, valid)

        @pl.when(valid <= FFN_SUB)
        def _():
            sub_blocks(slot, 1, valid)

        out_copy(g, slot).start()

    in_copy(0, 0).start()
    for c in weight_copies(blk_ref[BLK_EXPERT, 0]):
        c.start()

    def pair(p, carry):
        g = 2 * p
        block(g, 0)

        @pl.when(g + 1 < n_active)
        def _():
            block(g + 1, 1)

        return carry

    lax.fori_loop(0, lax.shift_right_logical(n_active + 1, 1), pair, 0)

    last = n_active - 1
    for slot in range(2):
        @pl.when((last >= slot) & (lax.rem(last - slot, 2) == 0))
        def _():
            out_copy(last, slot).wait()

        @pl.when((last >= 1) & (lax.rem(last - 1 - slot, 2) == 0))
        def _():
            out_copy(last - 1, slot).wait()


def _expert_ffn(blk, xg, w_gate, w_up, w_down, *, n_blk, layer):
    _, n_exp, d, hid = w_gate.shape
    half = d // 2
    kern = functools.partial(_ffn_kernel, half=half, layer=layer)
    any_spec = pl.BlockSpec(memory_space=pl.ANY)
    grid_spec = pltpu.PrefetchScalarGridSpec(
        num_scalar_prefetch=1,
        grid=(1,),
        in_specs=[any_spec, any_spec, any_spec, any_spec],
        out_specs=any_spec,
        scratch_shapes=[pltpu.VMEM((2, ROW_BLOCK, half), jnp.uint32), pltpu.VMEM((2, ROW_BLOCK, half), jnp.uint32),
                        pltpu.VMEM((d, hid), _F32), pltpu.VMEM((d, hid), _F32), pltpu.VMEM((hid, d), _F32),
                        pltpu.VMEM((d, hid), _BF16), pltpu.VMEM((d, hid), _BF16), pltpu.VMEM((hid, d), _BF16),
                        pltpu.SemaphoreType.DMA((2,)), pltpu.SemaphoreType.DMA((2,)), pltpu.SemaphoreType.DMA((3,))],
    )
    return pl.pallas_call(
        kern,
        grid_spec=grid_spec,
        out_shape=jax.ShapeDtypeStruct((n_blk * ROW_BLOCK, half), jnp.uint32),
        compiler_params=_params(),
        name="moe_expert_ffn",
    )(blk, xg, w_gate, w_up, w_down)


def _combine_kernel(x_ref, yg_ref, gate_ref, sg_ref, su_ref, sd_ref, g_ref, b_ref, o_ref, gpad_ref, *,
                    tm, d, alpha):
    x = x_ref[...]
    xb = x.astype(_BF16)
    half = d // 2
    gpad_ref[...] = jnp.zeros_like(gpad_ref)
    gpad_ref[0:TOP_K, :] = gate_ref[...]
    gate_t = gpad_ref[...].T
    lo_acc = jnp.zeros((tm, half), _F32)
    hi_acc = jnp.zeros((tm, half), _F32)
    for k in range(TOP_K):
        lo, hi = _unpack_pair(yg_ref[k])
        w = gate_t[:, k:k + 1]
        lo_acc = lo_acc + w * lo
        hi_acc = hi_acc + w * hi
    hs = (jax.nn.silu(_dot(xb, sg_ref[...])) * _dot(xb, su_ref[...])).astype(_BF16)
    shared = _dot(hs, sd_ref[...])
    f = jnp.concatenate([lo_acc, hi_acc], axis=1) + shared
    o_ref[...] = _layer_norm(alpha * x + f, g_ref[...], b_ref[...])


def _combine(x1, yg, gate, sh_gate, sh_up, sh_down, ln_g, ln_b, *, alpha, tm=256):
    n, d = x1.shape
    tm = min(tm, n)
    hid = sh_gate.shape[1]
    kern = functools.partial(_combine_kernel, tm=tm, d=d, alpha=alpha)
    return pl.pallas_call(
        kern,
        grid=(n // tm,),
        in_specs=[_rows(tm, d), pl.BlockSpec((TOP_K, tm, d // 2), lambda i: (0, i, 0)),
                  pl.BlockSpec((TOP_K, tm), lambda i: (0, i)), _full((d, hid)), _full((d, hid)),
                  _full((hid, d)), _full((1, d)), _full((1, d))],
        out_specs=_rows(tm, d),
        out_shape=jax.ShapeDtypeStruct((n, d), _F32),
        scratch_shapes=[pltpu.VMEM((LANES, tm), _F32)],
        compiler_params=_params(),
        name="moe_combine",
    )(x1, yg, gate, sh_gate.astype(_BF16), sh_up.astype(_BF16), sh_down.astype(_BF16),
      ln_g.reshape(1, d), ln_b.reshape(1, d))


SC_CORES = 2
SC_SUBCORES = 16
SC_WORKERS = SC_CORES * SC_SUBCORES
SC_WINDOW = 64


def _sc_mesh():
    return plsc.VectorSubcoreMesh(core_axis_name="c", subcore_axis_name="s",
                                  num_cores=SC_CORES, num_subcores=SC_SUBCORES)


def _sc_worker_id():
    return lax.axis_index("s") * SC_CORES + lax.axis_index("c")


def _dispatch_rows(xp, dest, n_rows):
    n, width = xp.shape
    n_slots = dest.shape[0]
    n_win = n // SC_WINDOW
    per_worker = n_win // SC_WORKERS
    assert per_worker * SC_WORKERS * SC_WINDOW == n and per_worker % 2 == 0
    dest_w = dest.reshape(n_slots, n_win, SC_WINDOW).transpose(1, 0, 2)

    def body(x_hbm, d_hbm, out_hbm, idx_v, rows_v, row_sem, idx_sem, out_sem):
        first = _sc_worker_id() * per_worker

        def load_rows(j, b):
            return pltpu.make_async_copy(x_hbm.at[pl.ds((first + j) * SC_WINDOW, SC_WINDOW)], rows_v.at[b],
                                         row_sem.at[b])

        def load_idx(j, b):
            return pltpu.make_async_copy(d_hbm.at[first + j], idx_v.at[b], idx_sem.at[b])

        def scatter(b, k):
            return pltpu.make_async_copy(rows_v.at[b], out_hbm.at[idx_v.at[b].at[k]], out_sem.at[b])

        for b in range(2):
            load_rows(b, b).start()
            load_idx(b, b).start()

        @pl.loop(0, per_worker, step=2)
        def _(j0):
            for b in range(2):
                j = j0 + b
                load_rows(j, b).wait()
                load_idx(j, b).wait()
                for k in range(n_slots):
                    scatter(b, k).start()
                for k in range(n_slots):
                    scatter(b, k).wait()

                @pl.when(j + 2 < per_worker)
                def _():
                    load_rows(j + 2, b).start()
                    load_idx(j + 2, b).start()

    return pl.kernel(
        body,
        out_type=jax.ShapeDtypeStruct((n_rows, width), xp.dtype),
        mesh=_sc_mesh(),
        scratch_types=[pltpu.VMEM((2, n_slots, SC_WINDOW), jnp.int32), pltpu.VMEM((2, SC_WINDOW, width), xp.dtype),
                       pltpu.SemaphoreType.DMA((2,)), pltpu.SemaphoreType.DMA((2,)), pltpu.SemaphoreType.DMA((2,))],
        name="moe_dispatch_rows",
    )(xp, dest_w)


def _gather_rows(yp, dest):
    n_slots, n = dest.shape
    width = yp.shape[1]
    m = n_slots * n
    per_worker = m // SC_WORKERS
    n_win = per_worker // SC_WINDOW
    assert n_win * SC_WINDOW * SC_WORKERS == m and n_win % 2 == 0

    def body(y_hbm, idx_hbm, out_hbm, idx_v, rows_v, in_sem, out_sem):
        base = _sc_worker_id() * per_worker
        pltpu.sync_copy(idx_hbm.at[pl.ds(base, per_worker)], idx_v)

        def gather(w, b):
            return pltpu.make_async_copy(y_hbm.at[idx_v.at[pl.ds(w * SC_WINDOW, SC_WINDOW)]], rows_v.at[b],
                                         in_sem.at[b])

        def put(w, b):
            return pltpu.make_async_copy(rows_v.at[b], out_hbm.at[pl.ds(base + w * SC_WINDOW, SC_WINDOW)],
                                         out_sem.at[b])

        for b in range(2):
            gather(b, b).start()

        @pl.loop(0, n_win, step=2)
        def _(w0):
            for b in range(2):
                w = w0 + b
                gather(w, b).wait()
                put(w, b).start()

                @pl.when(w >= 1)
                def _():
                    put(w - 1, 1 - b).wait()

                    @pl.when(w + 1 < n_win)
                    def _():
                        gather(w + 1, 1 - b).start()

        put(n_win - 1, 1).wait()

    out = pl.kernel(
        body,
        out_type=jax.ShapeDtypeStruct((m, width), yp.dtype),
        mesh=_sc_mesh(),
        scratch_types=[pltpu.VMEM((per_worker,), jnp.int32), pltpu.VMEM((2, SC_WINDOW, width), yp.dtype),
                       pltpu.SemaphoreType.DMA((2,)), pltpu.SemaphoreType.DMA((2,))],
        name="moe_gather_rows",
    )(yp, dest.reshape(m))
    return out.reshape(n_slots, n, width)


def _moe(x1, x1p, router_w, router_bias, w_gate, w_up, w_down, sh_gate, sh_up, sh_down, ln_g, ln_b, *, alpha,
         layer):
    n, d = x1.shape
    n_exp = router_w.shape[1]
    n_blk = (n * TOP_K + n_exp * (ROW_BLOCK - 1) + ROW_BLOCK - 1) // ROW_BLOCK
    eidx, gate, rank, cnt = _router(x1, router_w, router_bias)
    dest, blk = _tables(cnt, eidx, rank, n_blk=n_blk)
    xg = _dispatch_rows(x1p, dest, n_blk * ROW_BLOCK)
    y = _expert_ffn(blk, xg, w_gate, w_up, w_down, n_blk=n_blk, layer=layer)
    yg = _gather_rows(y, dest)
    return _combine(x1, yg, gate, sh_gate, sh_up, sh_down, ln_g, ln_b, alpha=alpha)


def kernel(x, a_w_in, a_conv_w, a_w_out, b_w_in, b_v_ln_g, b_v_ln_b, b_w_s, b_s_bias, b_w_out, c_w_in, c_w_grp,
           c_scale, c_w_out, d_w_in, d_conv_w, d_conv_b, d_ln_g, d_ln_b, d_w_out, ln1_g, ln1_b, ln2_g, ln2_b,
           router_w, router_bias, exp_w_gate, exp_w_up, exp_w_down, sh_w_gate, sh_w_up, sh_w_down):
    bsz, seq, d = x.shape
    depth = ln1_g.shape[0]
    n_mixers = 4
    alpha = (2 * depth) ** 0.25
    h = x.reshape(bsz * seq, d)
    for i in range(depth):
        m, j = i % n_mixers, i // n_mixers
        if m == 0:
            x1, x1p = _mixer_a(h, a_w_in[j], a_conv_w[j], a_w_out[j], ln1_g[i], ln1_b[i], seq=seq, alpha=alpha)
        elif m == 1:
            x1, x1p = _mixer_b(h, b_w_in[j], b_v_ln_g[j], b_v_ln_b[j], b_w_s[j], b_s_bias[j], b_w_out[j],
                               ln1_g[i], ln1_b[i], alpha=alpha)
        elif m == 2:
            x1, x1p = _mixer_c(h, c_w_in[j], c_w_grp[j], c_scale[j], c_w_out[j], ln1_g[i], ln1_b[i],
                               seq=seq, alpha=alpha)
        else:
            x1, x1p = _mixer_d(h, d_w_in[j], d_conv_w[j], d_conv_b[j], d_ln_g[j], d_ln_b[j], d_w_out[j],
                               ln1_g[i], ln1_b[i], seq=seq, alpha=alpha)
        h = _moe(x1, x1p, router_w[i], router_bias[i], exp_w_gate, exp_w_up, exp_w_down,
                 sh_w_gate[i], sh_w_up[i], sh_w_down[i], ln2_g[i], ln2_b[i], alpha=alpha, layer=i)
    return h.reshape(bsz, seq, d)
```

```python
import functools

import jax
import jax.numpy as jnp
from jax import lax
from jax.experimental import pallas as pl
from jax.experimental.pallas import tpu as pltpu
from jax.experimental.pallas import tpu_sc as plsc

LN_EPS = 1e-5
CHUNK = 128
GMLP_HEADS = 8
POOL_WINDOWS = (2, 4, 8, 16)
N_EXPERT_GROUPS = 8
TOPK_GROUPS = 4
TOP_K = 8
ROUTED_SCALE = 2.5
ROW_BLOCK = 512
FFN_SUB = 256
BLK_EXPERT, BLK_VALID, BLK_REGION_END, BLK_N_ACTIVE, BLK_ROWS_USED = 0, 1, 2, 3, 4
LANES = 128
SUBLANES = 8
VMEM_LIMIT = 56 * 1024 * 1024

_BF16 = jnp.bfloat16
_F32 = jnp.float32


def _dot(a, b):
    return jnp.dot(a, b, preferred_element_type=_F32)


def _layer_norm(z, g, b):
    mu = jnp.mean(z, axis=-1, keepdims=True)
    zc = z - mu
    var = jnp.mean(zc * zc, axis=-1, keepdims=True)
    return zc * lax.rsqrt(var + LN_EPS) * g + b


def _pack_pair(lo, hi):
    lo_bits = lax.bitcast_convert_type(lo.astype(_BF16).astype(_F32), jnp.uint32)
    hi_bits = lax.bitcast_convert_type(hi.astype(_BF16).astype(_F32), jnp.uint32)
    return lax.shift_right_logical(lo_bits, jnp.uint32(16)) | (hi_bits & jnp.uint32(0xFFFF0000))


def _unpack_pair(p):
    lo = lax.bitcast_convert_type(lax.shift_left(p, jnp.uint32(16)), _F32)
    hi = lax.bitcast_convert_type(p & jnp.uint32(0xFFFF0000), _F32)
    return lo, hi


def _store_residual_ln(x_ref, y, rows, g_ref, b_ref, alpha, o_ref, op_ref):
    x1 = _layer_norm(alpha * x_ref[rows, :] + y, g_ref[...], b_ref[...])
    o_ref[rows, :] = x1
    half = x1.shape[-1] // 2
    op_ref[rows, :] = _pack_pair(x1[:, :half], x1[:, half:])


def _params(n_axes=1):
    return pltpu.CompilerParams(dimension_semantics=("arbitrary",) * n_axes, vmem_limit_bytes=VMEM_LIMIT)


def _full(shape):
    nd = len(shape)
    return pl.BlockSpec(shape, lambda i, _nd=nd: (0,) * _nd, pipeline_mode=pl.Buffered(1))


def _rows(tm, width):
    return pl.BlockSpec((tm, width), lambda i: (i, 0))


def _mixer_out_shapes(n, d):
    return (jax.ShapeDtypeStruct((n, d), _F32), jax.ShapeDtypeStruct((n, d // 2), jnp.uint32))


def _mixer_a_kernel(x_ref, win_ref, cw_ref, wout_ref, g_ref, b_ref, o_ref, op_ref, zext_ref, *,
                    tm, d, tiles_per_seq, alpha, halo, row_group):
    i = pl.program_id(0)

    @pl.when(i % tiles_per_seq == 0)
    def _():
        zext_ref[0:halo, :] = jnp.zeros((halo, d), _F32)

    @pl.when(i % tiles_per_seq != 0)
    def _():
        zext_ref[0:halo, :] = zext_ref[tm:tm + halo, :]

    groups = range(0, tm, row_group)
    gates = []
    for g0 in groups:
        xb = x_ref[g0:g0 + row_group, :].astype(_BF16)
        gates.append(_dot(xb, win_ref[:, 0:d]))
        zext_ref[halo + g0:halo + g0 + row_group, :] = (
            _dot(xb, win_ref[:, d:2 * d]) * _dot(xb, win_ref[:, 2 * d:3 * d]))
    width = cw_ref.shape[0]
    for g0, b_gate in zip(groups, gates):
        conv = jnp.zeros((row_group, d), _F32)
        for k in range(width):
            off = halo - (width - 1) + k + g0
            conv = conv + cw_ref[k:k + 1, :] * zext_ref[off:off + row_group, :]
        y = _dot((b_gate * conv).astype(_BF16), wout_ref[...])
        rows = slice(g0, g0 + row_group)
        _store_residual_ln(x_ref, y, rows, g_ref, b_ref, alpha, o_ref, op_ref)


def _mixer_a(x2, w_in, conv_w, w_out, ln_g, ln_b, *, seq, alpha, tm=512, row_group=256):
    n, d = x2.shape
    tm = min(tm, seq)
    row_group = min(row_group, tm)
    halo = 8
    assert conv_w.shape[0] - 1 <= halo
    kern = functools.partial(_mixer_a_kernel, tm=tm, d=d, tiles_per_seq=seq // tm, alpha=alpha, halo=halo,
                             row_group=row_group)
    return pl.pallas_call(
        kern,
        grid=(n // tm,),
        in_specs=[_rows(tm, d), _full(w_in.shape), _full(conv_w.shape), _full(w_out.shape),
                  _full((1, d)), _full((1, d))],
        out_specs=(_rows(tm, d), _rows(tm, d // 2)),
        out_shape=_mixer_out_shapes(n, d),
        scratch_shapes=[pltpu.VMEM((tm + halo, d), _F32)],
        compiler_params=_params(),
        name="mixer_a",
    )(x2, w_in.astype(_BF16), conv_w, w_out.astype(_BF16), ln_g.reshape(1, d), ln_b.reshape(1, d))


def _mixer_b_kernel(x_ref, win_ref, vg_ref, vb_ref, ws_ref, sbt_ref, wout_ref, g_ref, b_ref, o_ref, op_ref,
                    gated_ref, *, tm, half, heads, alpha, row_group):
    groups = [slice(r, r + row_group) for r in range(0, tm, row_group)]
    us, vns = [], []
    for rows in groups:
        xb = x_ref[rows, :].astype(_BF16)
        us.append(jax.nn.gelu(_dot(xb, win_ref[:, 0:half])))
        v = jax.nn.gelu(_dot(xb, win_ref[:, half:2 * half]))
        vns.append(_layer_norm(v, vg_ref[...], vb_ref[...]).astype(_BF16))
    hd = half // heads
    row = lax.broadcasted_iota(jnp.int32, (CHUNK, CHUNK), 0)
    col = lax.broadcasted_iota(jnp.int32, (CHUNK, CHUNK), 1)
    causal = row >= col
    for h in range(heads):
        w_c = jnp.where(causal, ws_ref[h], 0.0).astype(_BF16)
        bias = sbt_ref[:, h:h + 1]
        cs = slice(h * hd, (h + 1) * hd)
        for rows, u, vn in zip(groups, us, vns):
            for c in range(row_group // CHUNK):
                rs = slice(c * CHUNK, (c + 1) * CHUNK)
                s = _dot(w_c, vn[rs, cs]) + bias
                gated_ref[rows.start + c * CHUNK:rows.start + (c + 1) * CHUNK, cs] = (u[rs, cs] * s).astype(_BF16)
    for rows in groups:
        y = _dot(gated_ref[rows, :], wout_ref[...])
        _store_residual_ln(x_ref, y, rows, g_ref, b_ref, alpha, o_ref, op_ref)


def _mixer_b(x2, w_in, v_ln_g, v_ln_b, w_s, s_bias, w_out, ln_g, ln_b, *, alpha, tm=512, row_group=256):
    n, d = x2.shape
    half = w_in.shape[1] // 2
    heads = w_s.shape[0]
    kern = functools.partial(_mixer_b_kernel, tm=tm, half=half, heads=heads, alpha=alpha, row_group=row_group)
    return pl.pallas_call(
        kern,
        grid=(n // tm,),
        in_specs=[_rows(tm, d), _full(w_in.shape), _full((1, half)), _full((1, half)), _full(w_s.shape),
                  _full((CHUNK, heads)), _full(w_out.shape), _full((1, d)), _full((1, d))],
        out_specs=(_rows(tm, d), _rows(tm, d // 2)),
        out_shape=_mixer_out_shapes(n, d),
        scratch_shapes=[pltpu.VMEM((tm, half), _BF16)],
        compiler_params=_params(),
        name="mixer_b",
    )(x2, w_in.astype(_BF16), v_ln_g.reshape(1, half), v_ln_b.reshape(1, half), w_s, s_bias.T,
      w_out.astype(_BF16), ln_g.reshape(1, d), ln_b.reshape(1, d))


def _mixer_c_kernel(x_ref, win_ref, wgrp_ref, scale_ref, wout_ref, g_ref, b_ref, o_ref, op_ref, hext_ref, *,
                    tm, d, tiles_per_seq, alpha, halo, row_group):
    i = pl.program_id(0)

    @pl.when(i % tiles_per_seq == 0)
    def _():
        hext_ref[0:halo, :] = jnp.zeros((halo, d), _F32)

    @pl.when(i % tiles_per_seq != 0)
    def _():
        hext_ref[0:halo, :] = hext_ref[tm:tm + halo, :]

    groups = range(0, tm, row_group)
    for g0 in groups:
        hext_ref[halo + g0:halo + g0 + row_group, :] = _dot(x_ref[g0:g0 + row_group, :].astype(_BF16), win_ref[...])
    gw = d // len(POOL_WINDOWS)
    for g0 in groups:
        pos = (i % tiles_per_seq) * tm + g0 + lax.broadcasted_iota(jnp.int32, (row_group, 1), 0) + 1
        y = jnp.zeros((row_group, d), _F32)
        for g, win in enumerate(POOL_WINDOWS):
            cs = slice(g * gw, (g + 1) * gw)
            h = hext_ref[halo + g0:halo + g0 + row_group, cs]
            acc = h
            for j in range(1, win):
                acc = acc + hext_ref[halo + g0 - j:halo + g0 - j + row_group, cs]
            mean = acc / jnp.minimum(pos, win).astype(_F32)
            p = mean - h
            yg = _dot(p.astype(_BF16), wgrp_ref[g]) * scale_ref[:, cs]
            y = y + _dot(yg.astype(_BF16), wout_ref[cs, :])
        rows = slice(g0, g0 + row_group)
        _store_residual_ln(x_ref, y, rows, g_ref, b_ref, alpha, o_ref, op_ref)


def _mixer_c(x2, w_in, w_grp, scale, w_out, ln_g, ln_b, *, seq, alpha, tm=512, row_group=256):
    n, d = x2.shape
    tm = min(tm, seq)
    row_group = min(row_group, tm)
    halo = 16
    assert max(POOL_WINDOWS) <= halo
    kern = functools.partial(_mixer_c_kernel, tm=tm, d=d, tiles_per_seq=seq // tm, alpha=alpha, halo=halo,
                             row_group=row_group)
    return pl.pallas_call(
        kern,
        grid=(n // tm,),
        in_specs=[_rows(tm, d), _full(w_in.shape), _full(w_grp.shape), _full((1, d)), _full(w_out.shape),
                  _full((1, d)), _full((1, d))],
        out_specs=(_rows(tm, d), _rows(tm, d // 2)),
        out_shape=_mixer_out_shapes(n, d),
        scratch_shapes=[pltpu.VMEM((tm + halo, d), _F32)],
        compiler_params=_params(),
        name="mixer_c",
    )(x2, w_in.astype(_BF16), w_grp.astype(_BF16), scale.reshape(1, d), w_out.astype(_BF16),
      ln_g.reshape(1, d), ln_b.reshape(1, d))


def _mixer_d_kernel(x_ref, win_ref, cw_ref, cb_ref, cg_ref, cbeta_ref, wout_ref, g_ref, b_ref, o_ref, op_ref,
                    hext_ref, shift_ref, conv_ref, *, tm, d, tiles_per_seq, alpha, halo, row_chunk, lane_chunk,
                    row_group):
    i = pl.program_id(0)

    @pl.when(i % tiles_per_seq == 0)
    def _():
        hext_ref[0:halo, :] = jnp.zeros((halo, d), _F32)

    @pl.when(i % tiles_per_seq != 0)
    def _():
        hext_ref[0:halo, :] = hext_ref[tm:tm + halo, :]

    groups = range(0, tm, row_group)
    for g0 in groups:
        xb = x_ref[g0:g0 + row_group, :].astype(_BF16)
        hext_ref[halo + g0:halo + g0 + row_group, :] = (
            _dot(xb, win_ref[:, 0:d]) * jax.nn.sigmoid(_dot(xb, win_ref[:, d:2 * d])))
    width = cw_ref.shape[0]
    shift_rows = row_group + halo - SUBLANES
    for g0 in groups:
        for r in range(1, SUBLANES):
            shift_ref[r - 1, g0:g0 + shift_rows, :] = hext_ref[g0 + r:g0 + r + shift_rows, :]
        for lc in range(d // lane_chunk):
            cs = slice(lc * lane_chunk, (lc + 1) * lane_chunk)
            bias = cb_ref[:, cs]
            for r0 in range(g0, g0 + row_group, row_chunk):
                acc = jnp.zeros((row_chunk, lane_chunk), _F32) + bias
                for k in range(width):
                    q, r = divmod(halo - (width - 1) + k, SUBLANES)
                    off = q * SUBLANES + r0
                    if r == 0:
                        src = hext_ref[off:off + row_chunk, cs]
                    else:
                        src = shift_ref[r - 1, off:off + row_chunk, cs]
                    acc = acc + cw_ref[k:k + 1, cs] * src
                conv_ref[r0:r0 + row_chunk, cs] = acc
    for g0 in groups:
        rows = slice(g0, g0 + row_group)
        c = _layer_norm(conv_ref[rows, :], cg_ref[...], cbeta_ref[...])
        y = _dot(jax.nn.silu(c).astype(_BF16), wout_ref[...])
        _store_residual_ln(x_ref, y, rows, g_ref, b_ref, alpha, o_ref, op_ref)


def _mixer_d(x2, w_in, conv_w, conv_b, c_ln_g, c_ln_b, w_out, ln_g, ln_b, *, seq, alpha, tm=512, row_group=256):
    n, d = x2.shape
    tm = min(tm, seq)
    row_group = min(row_group, tm)
    halo = 32
    assert conv_w.shape[0] - 1 <= halo
    kern = functools.partial(_mixer_d_kernel, tm=tm, d=d, tiles_per_seq=seq // tm, alpha=alpha, halo=halo,
                             row_chunk=64, lane_chunk=LANES, row_group=row_group)
    return pl.pallas_call(
        kern,
        grid=(n // tm,),
        in_specs=[_rows(tm, d), _full(w_in.shape), _full(conv_w.shape), _full((1, d)), _full((1, d)),
                  _full((1, d)), _full(w_out.shape), _full((1, d)), _full((1, d))],
        out_specs=(_rows(tm, d), _rows(tm, d // 2)),
        out_shape=_mixer_out_shapes(n, d),
        scratch_shapes=[pltpu.VMEM((tm + halo, d), _F32),
                        pltpu.VMEM((SUBLANES - 1, tm + halo - SUBLANES, d), _F32),
                        pltpu.VMEM((tm, d), _F32)],
        compiler_params=_params(),
        name="mixer_d",
    )(x2, w_in.astype(_BF16), conv_w, conv_b.reshape(1, d), c_ln_g.reshape(1, d), c_ln_b.reshape(1, d),
      w_out.astype(_BF16), ln_g.reshape(1, d), ln_b.reshape(1, d))


def _router_kernel(x_ref, wt_ref, bias_ref, eidx_ref, gate_ref, rank_ref, cnt_ref,
                   carry_ref, gs_ref, masked_ref, graw_ref, *, tm, n_exp):
    i = pl.program_id(0)

    @pl.when(i == 0)
    def _():
        carry_ref[...] = jnp.zeros_like(carry_ref)

    def split(v):
        hi = v.astype(_BF16)
        return hi, (v - hi.astype(_F32)).astype(_BF16)

    def dot_nt(a, b):
        return lax.dot_general(a, b, (((1,), (1,)), ((), ())), preferred_element_type=_F32)

    w_hi, w_lo = split(wt_ref[...])
    x_hi, x_lo = split(x_ref[...])
    logits = dot_nt(w_hi, x_hi) + (dot_nt(w_hi, x_lo) + dot_nt(w_lo, x_hi))
    scores = jax.nn.sigmoid(logits)
    biased = scores + bias_ref[...]
    per = n_exp // N_EXPERT_GROUPS
    neg = -jnp.inf
    sub = lax.broadcasted_iota(jnp.int32, (per, tm), 0)
    for g in range(N_EXPERT_GROUPS):
        bg = biased[g * per:(g + 1) * per, :]
        m1 = jnp.max(bg, axis=0, keepdims=True)
        first = jnp.min(jnp.where(bg == m1, sub, per), axis=0, keepdims=True)
        m2 = jnp.max(jnp.where(sub == first, neg, bg), axis=0, keepdims=True)
        gs_ref[g:g + 1, :] = m1 + m2

    giota = lax.broadcasted_iota(jnp.int32, (N_EXPERT_GROUPS, tm), 0)
    cur = gs_ref[...]
    gsel = jnp.zeros((N_EXPERT_GROUPS, tm), jnp.bool_)
    for _ in range(TOPK_GROUPS):
        m = jnp.max(cur, axis=0, keepdims=True)
        gi = jnp.min(jnp.where(cur == m, giota, N_EXPERT_GROUPS), axis=0, keepdims=True)
        pick = giota == gi
        gsel = gsel | pick
        cur = jnp.where(pick, neg, cur)
    gs_ref[...] = gsel.astype(_F32)
    for g in range(N_EXPERT_GROUPS):
        keep = gs_ref[g:g + 1, :] > 0.5
        masked_ref[g * per:(g + 1) * per, :] = jnp.where(keep, biased[g * per:(g + 1) * per, :], neg)

    eiota = lax.broadcasted_iota(jnp.int32, (n_exp, tm), 0)
    cur = masked_ref[...]
    sel = jnp.zeros((n_exp, tm), jnp.bool_)
    for k in range(TOP_K):
        m = jnp.max(cur, axis=0, keepdims=True)
        ei = jnp.min(jnp.where(cur == m, eiota, n_exp), axis=0, keepdims=True)
        pick = eiota == ei
        eidx_ref[k:k + 1, :] = ei
        graw_ref[k:k + 1, :] = jnp.sum(jnp.where(pick, scores, 0.0), axis=0, keepdims=True)
        sel = sel | pick
        cur = jnp.where(pick, neg, cur)
    graw = graw_ref[...]
    gate_ref[...] = graw / jnp.sum(graw, axis=0, keepdims=True) * ROUTED_SCALE

    sel_b = sel.astype(_F32).astype(_BF16)
    before = (lax.broadcasted_iota(jnp.int32, (tm, tm), 0)
              < lax.broadcasted_iota(jnp.int32, (tm, tm), 1)).astype(_F32).astype(_BF16)
    rank_all = _dot(sel_b, before) + carry_ref[:, 0:1]
    for k in range(TOP_K):
        pick = eiota == eidx_ref[k:k + 1, :]
        rank_ref[k:k + 1, :] = jnp.sum(jnp.where(pick, rank_all, 0.0), axis=0, keepdims=True).astype(jnp.int32)
    carry_ref[...] = carry_ref[...] + jnp.sum(sel.astype(_F32), axis=1, keepdims=True)
    cnt_ref[...] = carry_ref[...]


def _router(x1, router_w, router_bias, *, tm=512):
    n, d = x1.shape
    n_exp = router_w.shape[1]
    tm = min(tm, n)
    kern = functools.partial(_router_kernel, tm=tm, n_exp=n_exp)
    tok = pl.BlockSpec((TOP_K, tm), lambda i: (0, i))
    return pl.pallas_call(
        kern,
        grid=(n // tm,),
        in_specs=[_rows(tm, d), _full((n_exp, d)), _full((n_exp, 1))],
        out_specs=(tok, tok, tok, pl.BlockSpec((n_exp, LANES), lambda i: (0, 0))),
        out_shape=(jax.ShapeDtypeStruct((TOP_K, n), jnp.int32), jax.ShapeDtypeStruct((TOP_K, n), _F32),
                   jax.ShapeDtypeStruct((TOP_K, n), jnp.int32), jax.ShapeDtypeStruct((n_exp, LANES), _F32)),
        scratch_shapes=[pltpu.VMEM((n_exp, LANES), _F32), pltpu.VMEM((N_EXPERT_GROUPS, tm), _F32),
                        pltpu.VMEM((n_exp, tm), _F32), pltpu.VMEM((TOP_K, tm), _F32)],
        compiler_params=_params(),
        name="moe_router",
    )(x1, router_w.T, router_bias.reshape(n_exp, 1))


def _tables_kernel(cnt_ref, eidx_ref, rank_ref, dest_ref, blk_ref, *, n_exp, n_blk_pad):
    counts = cnt_ref[...].astype(jnp.int32)
    shift = ROW_BLOCK.bit_length() - 1
    padded = lax.shift_left(lax.shift_right_logical(counts + (ROW_BLOCK - 1), shift), shift)
    lower = (lax.broadcasted_iota(jnp.int32, (n_exp, n_exp), 1)
             < lax.broadcasted_iota(jnp.int32, (n_exp, n_exp), 0)).astype(_F32)
    pad_start = jnp.dot(lower, padded.astype(_F32), precision=lax.Precision.HIGHEST,
                        preferred_element_type=_F32).astype(jnp.int32)
    pad_end = pad_start + padded
    eidx = eidx_ref[...]
    dest = rank_ref[...]
    for e in range(n_exp):
        dest = dest + jnp.where(eidx == e, pad_start[e:e + 1, 0:1], 0)
    dest_ref[...] = dest

    @pl.when(pl.program_id(0) == 0)
    def _():
        bstart = lax.broadcasted_iota(jnp.int32, (1, n_blk_pad), 1) * ROW_BLOCK
        blk_e = jnp.zeros((1, n_blk_pad), jnp.int32)
        for e in range(n_exp):
            blk_e = blk_e + (pad_end[e:e + 1, 0:1] <= bstart).astype(jnp.int32)
        blk_e = jnp.minimum(blk_e, n_exp - 1)
        used_end = pad_start + counts
        valid = jnp.zeros((1, n_blk_pad), jnp.int32)
        for e in range(n_exp):
            valid = valid + jnp.where(blk_e == e, used_end[e:e + 1, 0:1], 0)
        valid = jnp.clip(valid - bstart, 0, ROW_BLOCK)
        end_blk = lax.shift_right_logical(pad_end, shift)
        region_end = jnp.zeros((1, n_blk_pad), jnp.int32)
        for e in range(n_exp):
            region_end = region_end + jnp.where(blk_e == e, end_blk[e:e + 1, 0:1], 0)
        blk_ref[BLK_EXPERT:BLK_EXPERT + 1, :] = blk_e
        blk_ref[BLK_VALID:BLK_VALID + 1, :] = valid
        blk_ref[BLK_REGION_END:BLK_REGION_END + 1, :] = region_end
        blk_ref[BLK_N_ACTIVE:BLK_N_ACTIVE + 1, :] = jnp.broadcast_to(end_blk[n_exp - 1:n_exp, 0:1], (1, n_blk_pad))
        blk_ref[BLK_ROWS_USED:8, :] = jnp.zeros((8 - BLK_ROWS_USED, n_blk_pad), jnp.int32)


def _tables(cnt, eidx, rank, *, n_blk, tm=2048):
    n = eidx.shape[1]
    n_exp = cnt.shape[0]
    tm = min(tm, n)
    n_blk_pad = pl.cdiv(n_blk, LANES) * LANES
    kern = functools.partial(_tables_kernel, n_exp=n_exp, n_blk_pad=n_blk_pad)
    tok = pl.BlockSpec((TOP_K, tm), lambda i: (0, i))
    return pl.pallas_call(
        kern,
        grid=(n // tm,),
        in_specs=[_full((n_exp, LANES)), tok, tok],
        out_specs=(tok, pl.BlockSpec((8, n_blk_pad), lambda i: (0, 0))),
        out_shape=(jax.ShapeDtypeStruct((TOP_K, n), jnp.int32), jax.ShapeDtypeStruct((8, n_blk_pad), jnp.int32)),
        compiler_params=_params(),
        name="moe_tables",
    )(cnt, eidx, rank)


def _ffn_kernel(blk_ref, xg_hbm, wg_hbm, wu_hbm, wd_hbm, y_hbm,
                xbuf, ybuf, wg_f, wu_f, wd_f, wg_s, wu_s, wd_s, in_sem, out_sem, w_sem, *, half, layer):
    n_active = blk_ref[BLK_N_ACTIVE, 0]

    def in_copy(g, slot):
        return pltpu.make_async_copy(xg_hbm.at[pl.ds(g * ROW_BLOCK, ROW_BLOCK)], xbuf.at[slot], in_sem.at[slot])

    def out_copy(g, slot):
        return pltpu.make_async_copy(ybuf.at[slot], y_hbm.at[pl.ds(g * ROW_BLOCK, ROW_BLOCK)], out_sem.at[slot])

    def weight_copies(e):
        return (pltpu.make_async_copy(wg_hbm.at[layer, e], wg_f, w_sem.at[0]),
                pltpu.make_async_copy(wu_hbm.at[layer, e], wu_f, w_sem.at[1]),
                pltpu.make_async_copy(wd_hbm.at[layer, e], wd_f, w_sem.at[2]))

    def sub_blocks(slot, n_sub, valid):
        hidden = []
        for s in range(n_sub):
            rs = slice(s * FFN_SUB, (s + 1) * FFN_SUB)
            keep = lax.broadcasted_iota(jnp.int32, (FFN_SUB, 1), 0) + s * FFN_SUB < valid
            lo, hi = _unpack_pair(xbuf[slot, rs, :])
            lo = jnp.where(keep, lo, 0.0).astype(_BF16)
            hi = jnp.where(keep, hi, 0.0).astype(_BF16)
            g = _dot(lo, wg_s[0:half, :]) + _dot(hi, wg_s[half:2 * half, :])
            u = _dot(lo, wu_s[0:half, :]) + _dot(hi, wu_s[half:2 * half, :])
            hidden.append((jax.nn.silu(g) * u).astype(_BF16))
        for s, h in enumerate(hidden):
            rs = slice(s * FFN_SUB, (s + 1) * FFN_SUB)
            ybuf[slot, rs, :] = _pack_pair(_dot(h, wd_s[:, 0:half]), _dot(h, wd_s[:, half:2 * half]))

    def block(g, slot):
        e = blk_ref[BLK_EXPERT, g]
        valid = blk_ref[BLK_VALID, g]

        @pl.when(g + 1 < n_active)
        def _():
            in_copy(g + 1, 1 - slot).start()

        @pl.when((g == 0) | (e != blk_ref[BLK_EXPERT, jnp.maximum(g - 1, 0)]))
        def _():
            for c in weight_copies(e):
                c.wait()
            wg_s[...] = wg_f[...].astype(_BF16)
            wu_s[...] = wu_f[...].astype(_BF16)
            wd_s[...] = wd_f[...].astype(_BF16)
            nxt = blk_ref[BLK_REGION_END, g]

            @pl.when(nxt < n_active)
            def _():
                for c in weight_copies(blk_ref[BLK_EXPERT, nxt]):
                    c.start()

        in_copy(g, slot).wait()

        @pl.when(g >= 2)
        def _():
            out_copy(g - 2, slot).wait()

        @pl.when(valid > FFN_SUB)
        def _():
            sub_blocks(slot, ROW_BLOCK // FFN_SUB, valid)

        @pl.when(valid <= FFN_SUB)
        def _():
            sub_blocks(slot, 1, valid)

        out_copy(g, slot).start()

    in_copy(0, 0).start()
    for c in weight_copies(blk_ref[BLK_EXPERT, 0]):
        c.start()

    def pair(p, carry):
        g = 2 * p
        block(g, 0)

        @pl.when(g + 1 < n_active)
        def _():
            block(g + 1, 1)

        return carry

    lax.fori_loop(0, lax.shift_right_logical(n_active + 1, 1), pair, 0)

    last = n_active - 1
    for slot in range(2):
        @pl.when((last >= slot) & (lax.rem(last - slot, 2) == 0))
        def _():
            out_copy(last, slot).wait()

        @pl.when((last >= 1) & (lax.rem(last - 1 - slot, 2) == 0))
        def _():
            out_copy(last - 1, slot).wait()


def _expert_ffn(blk, xg, w_gate, w_up, w_down, *, n_blk, layer):
    _, n_exp, d, hid = w_gate.shape
    half = d // 2
    kern = functools.partial(_ffn_kernel, half=half, layer=layer)
    any_spec = pl.BlockSpec(memory_space=pl.ANY)
    grid_spec = pltpu.PrefetchScalarGridSpec(
        num_scalar_prefetch=1,
        grid=(1,),
        in_specs=[any_spec, any_spec, any_spec, any_spec],
        out_specs=any_spec,
        scratch_shapes=[pltpu.VMEM((2, ROW_BLOCK, half), jnp.uint32), pltpu.VMEM((2, ROW_BLOCK, half), jnp.uint32),
                        pltpu.VMEM((d, hid), _F32), pltpu.VMEM((d, hid), _F32), pltpu.VMEM((hid, d), _F32),
                        pltpu.VMEM((d, hid), _BF16), pltpu.VMEM((d, hid), _BF16), pltpu.VMEM((hid, d), _BF16),
                        pltpu.SemaphoreType.DMA((2,)), pltpu.SemaphoreType.DMA((2,)), pltpu.SemaphoreType.DMA((3,))],
    )
    return pl.pallas_call(
        kern,
        grid_spec=grid_spec,
        out_shape=jax.ShapeDtypeStruct((n_blk * ROW_BLOCK, half), jnp.uint32),
        compiler_params=_params(),
        name="moe_expert_ffn",
    )(blk, xg, w_gate, w_up, w_down)


def _combine_kernel(x_ref, yg_ref, gate_ref, sg_ref, su_ref, sd_ref, g_ref, b_ref, o_ref, gpad_ref, *,
                    tm, d, alpha):
    x = x_ref[...]
    xb = x.astype(_BF16)
    half = d // 2
    gpad_ref[...] = jnp.zeros_like(gpad_ref)
    gpad_ref[0:TOP_K, :] = gate_ref[...]
    gate_t = gpad_ref[...].T
    lo_acc = jnp.zeros((tm, half), _F32)
    hi_acc = jnp.zeros((tm, half), _F32)
    for k in range(TOP_K):
        lo, hi = _unpack_pair(yg_ref[k])
        w = gate_t[:, k:k + 1]
        lo_acc = lo_acc + w * lo
        hi_acc = hi_acc + w * hi
    hs = (jax.nn.silu(_dot(xb, sg_ref[...])) * _dot(xb, su_ref[...])).astype(_BF16)
    shared = _dot(hs, sd_ref[...])
    f = jnp.concatenate([lo_acc, hi_acc], axis=1) + shared
    o_ref[...] = _layer_norm(alpha * x + f, g_ref[...], b_ref[...])


def _combine(x1, yg, gate, sh_gate, sh_up, sh_down, ln_g, ln_b, *, alpha, tm=512):
    n, d = x1.shape
    tm = min(tm, n)
    hid = sh_gate.shape[1]
    kern = functools.partial(_combine_kernel, tm=tm, d=d, alpha=alpha)
    return pl.pallas_call(
        kern,
        grid=(n // tm,),
        in_specs=[_rows(tm, d), pl.BlockSpec((TOP_K, tm, d // 2), lambda i: (0, i, 0)),
                  pl.BlockSpec((TOP_K, tm), lambda i: (0, i)), _full((d, hid)), _full((d, hid)),
                  _full((hid, d)), _full((1, d)), _full((1, d))],
        out_specs=_rows(tm, d),
        out_shape=jax.ShapeDtypeStruct((n, d), _F32),
        scratch_shapes=[pltpu.VMEM((LANES, tm), _F32)],
        compiler_params=_params(),
        name="moe_combine",
    )(x1, yg, gate, sh_gate.astype(_BF16), sh_up.astype(_BF16), sh_down.astype(_BF16),
      ln_g.reshape(1, d), ln_b.reshape(1, d))


SC_CORES = 2
SC_SUBCORES = 16
SC_WORKERS = SC_CORES * SC_SUBCORES
SC_WINDOW = 64


def _sc_mesh():
    return plsc.VectorSubcoreMesh(core_axis_name="c", subcore_axis_name="s",
                                  num_cores=SC_CORES, num_subcores=SC_SUBCORES)


def _sc_worker_id():
    return lax.axis_index("s") * SC_CORES + lax.axis_index("c")


def _dispatch_rows(xp, dest, n_rows):
    n, width = xp.shape
    n_slots = dest.shape[0]
    n_win = n // SC_WINDOW
    per_worker = n_win // SC_WORKERS
    assert per_worker * SC_WORKERS * SC_WINDOW == n and per_worker % 2 == 0
    dest_w = dest.reshape(n_slots, n_win, SC_WINDOW).transpose(1, 0, 2)

    def body(x_hbm, d_hbm, out_hbm, idx_v, rows_v, row_sem, idx_sem, out_sem):
        first = _sc_worker_id() * per_worker

        def load_rows(j, b):
            return pltpu.make_async_copy(x_hbm.at[pl.ds((first + j) * SC_WINDOW, SC_WINDOW)], rows_v.at[b],
                                         row_sem.at[b])

        def load_idx(j, b):
            return pltpu.make_async_copy(d_hbm.at[first + j], idx_v.at[b], idx_sem.at[b])

        def scatter(b, k):
            return pltpu.make_async_copy(rows_v.at[b], out_hbm.at[idx_v.at[b].at[k]], out_sem.at[b])

        for b in range(2):
            load_rows(b, b).start()
            load_idx(b, b).start()

        @pl.loop(0, per_worker, step=2)
        def _(j0):
            for b in range(2):
                j = j0 + b
                load_rows(j, b).wait()
                load_idx(j, b).wait()
                for k in range(n_slots):
                    scatter(b, k).start()
                for k in range(n_slots):
                    scatter(b, k).wait()

                @pl.when(j + 2 < per_worker)
                def _():
                    load_rows(j + 2, b).start()
                    load_idx(j + 2, b).start()

    return pl.kernel(
        body,
        out_type=jax.ShapeDtypeStruct((n_rows, width), xp.dtype),
        mesh=_sc_mesh(),
        scratch_types=[pltpu.VMEM((2, n_slots, SC_WINDOW), jnp.int32), pltpu.VMEM((2, SC_WINDOW, width), xp.dtype),
                       pltpu.SemaphoreType.DMA((2,)), pltpu.SemaphoreType.DMA((2,)), pltpu.SemaphoreType.DMA((2,))],
        name="moe_dispatch_rows",
    )(xp, dest_w)


def _gather_rows(yp, dest):
    n_slots, n = dest.shape
    width = yp.shape[1]
    m = n_slots * n
    per_worker = m // SC_WORKERS
    n_win = per_worker // SC_WINDOW
    assert n_win * SC_WINDOW * SC_WORKERS == m and n_win % 2 == 0

    def body(y_hbm, idx_hbm, out_hbm, idx_v, rows_v, in_sem, out_sem):
        base = _sc_worker_id() * per_worker
        pltpu.sync_copy(idx_hbm.at[pl.ds(base, per_worker)], idx_v)

        def gather(w, b):
            return pltpu.make_async_copy(y_hbm.at[idx_v.at[pl.ds(w * SC_WINDOW, SC_WINDOW)]], rows_v.at[b],
                                         in_sem.at[b])

        def put(w, b):
            return pltpu.make_async_copy(rows_v.at[b], out_hbm.at[pl.ds(base + w * SC_WINDOW, SC_WINDOW)],
                                         out_sem.at[b])

        for b in range(2):
            gather(b, b).start()

        @pl.loop(0, n_win, step=2)
        def _(w0):
            for b in range(2):
                w = w0 + b
                gather(w, b).wait()
                put(w, b).start()

                @pl.when(w >= 1)
                def _():
                    put(w - 1, 1 - b).wait()

                    @pl.when(w + 1 < n_win)
                    def _():
                        gather(w + 1, 1 - b).start()

        put(n_win - 1, 1).wait()

    out = pl.kernel(
        body,
        out_type=jax.ShapeDtypeStruct((m, width), yp.dtype),
        mesh=_sc_mesh(),
        scratch_types=[pltpu.VMEM((per_worker,), jnp.int32), pltpu.VMEM((2, SC_WINDOW, width), yp.dtype),
                       pltpu.SemaphoreType.DMA((2,)), pltpu.SemaphoreType.DMA((2,))],
        name="moe_gather_rows",
    )(yp, dest.reshape(m))
    return out.reshape(n_slots, n, width)


def _moe(x1, x1p, router_w, router_bias, w_gate, w_up, w_down, sh_gate, sh_up, sh_down, ln_g, ln_b, *, alpha,
         layer):
    n, d = x1.shape
    n_exp = router_w.shape[1]
    n_blk = (n * TOP_K + n_exp * (ROW_BLOCK - 1) + ROW_BLOCK - 1) // ROW_BLOCK
    eidx, gate, rank, cnt = _router(x1, router_w, router_bias)
    dest, blk = _tables(cnt, eidx, rank, n_blk=n_blk)
    xg = _dispatch_rows(x1p, dest, n_blk * ROW_BLOCK)
    y = _expert_ffn(blk, xg, w_gate, w_up, w_down, n_blk=n_blk, layer=layer)
    yg = _gather_rows(y, dest)
    return _combine(x1, yg, gate, sh_gate, sh_up, sh_down, ln_g, ln_b, alpha=alpha)


def kernel(x, a_w_in, a_conv_w, a_w_out, b_w_in, b_v_ln_g, b_v_ln_b, b_w_s, b_s_bias, b_w_out, c_w_in, c_w_grp,
           c_scale, c_w_out, d_w_in, d_conv_w, d_conv_b, d_ln_g, d_ln_b, d_w_out, ln1_g, ln1_b, ln2_g, ln2_b,
           router_w, router_bias, exp_w_gate, exp_w_up, exp_w_down, sh_w_gate, sh_w_up, sh_w_down):
    bsz, seq, d = x.shape
    depth = ln1_g.shape[0]
    n_mixers = 4
    alpha = (2 * depth) ** 0.25
    h = x.reshape(bsz * seq, d)
    for i in range(depth):
        m, j = i % n_mixers, i // n_mixers
        if m == 0:
            x1, x1p = _mixer_a(h, a_w_in[j], a_conv_w[j], a_w_out[j], ln1_g[i], ln1_b[i], seq=seq, alpha=alpha)
        elif m == 1:
            x1, x1p = _mixer_b(h, b_w_in[j], b_v_ln_g[j], b_v_ln_b[j], b_w_s[j], b_s_bias[j], b_w_out[j],
                               ln1_g[i], ln1_b[i], alpha=alpha)
        elif m == 2:
            x1, x1p = _mixer_c(h, c_w_in[j], c_w_grp[j], c_scale[j], c_w_out[j], ln1_g[i], ln1_b[i],
                               seq=seq, alpha=alpha)
        else:
            x1, x1p = _mixer_d(h, d_w_in[j], d_conv_w[j], d_conv_b[j], d_ln_g[j], d_ln_b[j], d_w_out[j],
                               ln1_g[i], ln1_b[i], seq=seq, alpha=alpha)
        h = _moe(x1, x1p, router_w[i], router_bias[i], exp_w_gate, exp_w_up, exp_w_down,
                 sh_w_gate[i], sh_w_up[i], sh_w_down[i], ln2_g[i], ln2_b[i], alpha=alpha, layer=i)
    return h.reshape(bsz, seq, d)
```

```python
import functools

import jax
import jax.numpy as jnp
from jax import lax
from jax.experimental import pallas as pl
from jax.experimental.pallas import tpu as pltpu
from jax.experimental.pallas import tpu_sc as plsc

LN_EPS = 1e-5
CHUNK = 128
GMLP_HEADS = 8
POOL_WINDOWS = (2, 4, 8, 16)
N_EXPERT_GROUPS = 8
TOPK_GROUPS = 4
TOP_K = 8
ROUTED_SCALE = 2.5
ROW_BLOCK = 512
FFN_SUB = 256
BLK_EXPERT, BLK_VALID, BLK_REGION_END, BLK_N_ACTIVE, BLK_ROWS_USED = 0, 1, 2, 3, 4
LANES = 128
SUBLANES = 8
VMEM_LIMIT = 56 * 1024 * 1024

_BF16 = jnp.bfloat16
_F32 = jnp.float32


def _dot(a, b):
    return jnp.dot(a, b, preferred_element_type=_F32)


def _layer_norm(z, g, b):
    mu = jnp.mean(z, axis=-1, keepdims=True)
    zc = z - mu
    var = jnp.mean(zc * zc, axis=-1, keepdims=True)
    return zc * lax.rsqrt(var + LN_EPS) * g + b


def _pack_pair(lo, hi):
    lo_bits = lax.bitcast_convert_type(lo.astype(_BF16).astype(_F32), jnp.uint32)
    hi_bits = lax.bitcast_convert_type(hi.astype(_BF16).astype(_F32), jnp.uint32)
    return lax.shift_right_logical(lo_bits, jnp.uint32(16)) | (hi_bits & jnp.uint32(0xFFFF0000))


def _unpack_pair(p):
    lo = lax.bitcast_convert_type(lax.shift_left(p, jnp.uint32(16)), _F32)
    hi = lax.bitcast_convert_type(p & jnp.uint32(0xFFFF0000), _F32)
    return lo, hi


def _store_residual_ln(x_ref, y, rows, g_ref, b_ref, alpha, o_ref, op_ref):
    x1 = _layer_norm(alpha * x_ref[rows, :] + y, g_ref[...], b_ref[...])
    o_ref[rows, :] = x1
    half = x1.shape[-1] // 2
    op_ref[rows, :] = _pack_pair(x1[:, :half], x1[:, half:])


def _params(n_axes=1):
    return pltpu.CompilerParams(dimension_semantics=("arbitrary",) * n_axes, vmem_limit_bytes=VMEM_LIMIT)


def _full(shape):
    nd = len(shape)
    return pl.BlockSpec(shape, lambda i, _nd=nd: (0,) * _nd, pipeline_mode=pl.Buffered(1))


def _rows(tm, width):
    return pl.BlockSpec((tm, width), lambda i: (i, 0))


def _mixer_out_shapes(n, d):
    return (jax.ShapeDtypeStruct((n, d), _F32), jax.ShapeDtypeStruct((n, d // 2), jnp.uint32))


def _mixer_a_kernel(x_ref, win_ref, cw_ref, wout_ref, g_ref, b_ref, o_ref, op_ref, zext_ref, *,
                    tm, d, tiles_per_seq, alpha, halo, row_group):
    i = pl.program_id(0)

    @pl.when(i % tiles_per_seq == 0)
    def _():
        zext_ref[0:halo, :] = jnp.zeros((halo, d), _F32)

    @pl.when(i % tiles_per_seq != 0)
    def _():
        zext_ref[0:halo, :] = zext_ref[tm:tm + halo, :]

    groups = range(0, tm, row_group)
    gates = []
    for g0 in groups:
        xb = x_ref[g0:g0 + row_group, :].astype(_BF16)
        gates.append(_dot(xb, win_ref[:, 0:d]))
        zext_ref[halo + g0:halo + g0 + row_group, :] = (
            _dot(xb, win_ref[:, d:2 * d]) * _dot(xb, win_ref[:, 2 * d:3 * d]))
    width = cw_ref.shape[0]
    for g0, b_gate in zip(groups, gates):
        conv = jnp.zeros((row_group, d), _F32)
        for k in range(width):
            off = halo - (width - 1) + k + g0
            conv = conv + cw_ref[k:k + 1, :] * zext_ref[off:off + row_group, :]
        y = _dot((b_gate * conv).astype(_BF16), wout_ref[...])
        rows = slice(g0, g0 + row_group)
        _store_residual_ln(x_ref, y, rows, g_ref, b_ref, alpha, o_ref, op_ref)


def _mixer_a(x2, w_in, conv_w, w_out, ln_g, ln_b, *, seq, alpha, tm=512, row_group=256):
    n, d = x2.shape
    tm = min(tm, seq)
    row_group = min(row_group, tm)
    halo = 8
    assert conv_w.shape[0] - 1 <= halo
    kern = functools.partial(_mixer_a_kernel, tm=tm, d=d, tiles_per_seq=seq // tm, alpha=alpha, halo=halo,
                             row_group=row_group)
    return pl.pallas_call(
        kern,
        grid=(n // tm,),
        in_specs=[_rows(tm, d), _full(w_in.shape), _full(conv_w.shape), _full(w_out.shape),
                  _full((1, d)), _full((1, d))],
        out_specs=(_rows(tm, d), _rows(tm, d // 2)),
        out_shape=_mixer_out_shapes(n, d),
        scratch_shapes=[pltpu.VMEM((tm + halo, d), _F32)],
        compiler_params=_params(),
        name="mixer_a",
    )(x2, w_in.astype(_BF16), conv_w, w_out.astype(_BF16), ln_g.reshape(1, d), ln_b.reshape(1, d))


def _mixer_b_kernel(x_ref, win_ref, vg_ref, vb_ref, ws_ref, sbt_ref, wout_ref, g_ref, b_ref, o_ref, op_ref,
                    gated_ref, *, tm, half, heads, alpha, row_group):
    groups = [slice(r, r + row_group) for r in range(0, tm, row_group)]
    us, vns = [], []
    for rows in groups:
        xb = x_ref[rows, :].astype(_BF16)
        us.append(jax.nn.gelu(_dot(xb, win_ref[:, 0:half])))
        v = jax.nn.gelu(_dot(xb, win_ref[:, half:2 * half]))
        vns.append(_layer_norm(v, vg_ref[...], vb_ref[...]).astype(_BF16))
    hd = half // heads
    row = lax.broadcasted_iota(jnp.int32, (CHUNK, CHUNK), 0)
    col = lax.broadcasted_iota(jnp.int32, (CHUNK, CHUNK), 1)
    causal = row >= col
    for h in range(heads):
        w_c = jnp.where(causal, ws_ref[h], 0.0).astype(_BF16)
        bias = sbt_ref[:, h:h + 1]
        cs = slice(h * hd, (h + 1) * hd)
        for rows, u, vn in zip(groups, us, vns):
            for c in range(row_group // CHUNK):
                rs = slice(c * CHUNK, (c + 1) * CHUNK)
                s = _dot(w_c, vn[rs, cs]) + bias
                gated_ref[rows.start + c * CHUNK:rows.start + (c + 1) * CHUNK, cs] = (u[rs, cs] * s).astype(_BF16)
    for rows in groups:
        y = _dot(gated_ref[rows, :], wout_ref[...])
        _store_residual_ln(x_ref, y, rows, g_ref, b_ref, alpha, o_ref, op_ref)


def _mixer_b(x2, w_in, v_ln_g, v_ln_b, w_s, s_bias, w_out, ln_g, ln_b, *, alpha, tm=512, row_group=256):
    n, d = x2.shape
    half = w_in.shape[1] // 2
    heads = w_s.shape[0]
    kern = functools.partial(_mixer_b_kernel, tm=tm, half=half, heads=heads, alpha=alpha, row_group=row_group)
    return pl.pallas_call(
        kern,
        grid=(n // tm,),
        in_specs=[_rows(tm, d), _full(w_in.shape), _full((1, half)), _full((1, half)), _full(w_s.shape),
                  _full((CHUNK, heads)), _full(w_out.shape), _full((1, d)), _full((1, d))],
        out_specs=(_rows(tm, d), _rows(tm, d // 2)),
        out_shape=_mixer_out_shapes(n, d),
        scratch_shapes=[pltpu.VMEM((tm, half), _BF16)],
        compiler_params=_params(),
        name="mixer_b",
    )(x2, w_in.astype(_BF16), v_ln_g.reshape(1, half), v_ln_b.reshape(1, half), w_s, s_bias.T,
      w_out.astype(_BF16), ln_g.reshape(1, d), ln_b.reshape(1, d))


def _mixer_c_kernel(x_ref, win_ref, wgrp_ref, scale_ref, wout_ref, g_ref, b_ref, o_ref, op_ref, hext_ref, *,
                    tm, d, tiles_per_seq, alpha, halo, row_group):
    i = pl.program_id(0)

    @pl.when(i % tiles_per_seq == 0)
    def _():
        hext_ref[0:halo, :] = jnp.zeros((halo, d), _F32)

    @pl.when(i % tiles_per_seq != 0)
    def _():
        hext_ref[0:halo, :] = hext_ref[tm:tm + halo, :]

    groups = range(0, tm, row_group)
    for g0 in groups:
        hext_ref[halo + g0:halo + g0 + row_group, :] = _dot(x_ref[g0:g0 + row_group, :].astype(_BF16), win_ref[...])
    gw = d // len(POOL_WINDOWS)
    for g0 in groups:
        pos = (i % tiles_per_seq) * tm + g0 + lax.broadcasted_iota(jnp.int32, (row_group, 1), 0) + 1
        y = jnp.zeros((row_group, d), _F32)
        for g, win in enumerate(POOL_WINDOWS):
            cs = slice(g * gw, (g + 1) * gw)
            h = hext_ref[halo + g0:halo + g0 + row_group, cs]
            acc = h
            for j in range(1, win):
                acc = acc + hext_ref[halo + g0 - j:halo + g0 - j + row_group, cs]
            mean = acc / jnp.minimum(pos, win).astype(_F32)
            p = mean - h
            yg = _dot(p.astype(_BF16), wgrp_ref[g]) * scale_ref[:, cs]
            y = y + _dot(yg.astype(_BF16), wout_ref[cs, :])
        rows = slice(g0, g0 + row_group)
        _store_residual_ln(x_ref, y, rows, g_ref, b_ref, alpha, o_ref, op_ref)


def _mixer_c(x2, w_in, w_grp, scale, w_out, ln_g, ln_b, *, seq, alpha, tm=512, row_group=256):
    n, d = x2.shape
    tm = min(tm, seq)
    row_group = min(row_group, tm)
    halo = 16
    assert max(POOL_WINDOWS) <= halo
    kern = functools.partial(_mixer_c_kernel, tm=tm, d=d, tiles_per_seq=seq // tm, alpha=alpha, halo=halo,
                             row_group=row_group)
    return pl.pallas_call(
        kern,
        grid=(n // tm,),
        in_specs=[_rows(tm, d), _full(w_in.shape), _full(w_grp.shape), _full((1, d)), _full(w_out.shape),
                  _full((1, d)), _full((1, d))],
        out_specs=(_rows(tm, d), _rows(tm, d // 2)),
        out_shape=_mixer_out_shapes(n, d),
        scratch_shapes=[pltpu.VMEM((tm + halo, d), _F32)],
        compiler_params=_params(),
        name="mixer_c",
    )(x2, w_in.astype(_BF16), w_grp.astype(_BF16), scale.reshape(1, d), w_out.astype(_BF16),
      ln_g.reshape(1, d), ln_b.reshape(1, d))


def _mixer_d_kernel(x_ref, win_ref, cw_ref, cb_ref, cg_ref, cbeta_ref, wout_ref, g_ref, b_ref, o_ref, op_ref,
                    hext_ref, shift_ref, conv_ref, *, tm, d, tiles_per_seq, alpha, halo, row_chunk, lane_chunk,
                    row_group):
    i = pl.program_id(0)

    @pl.when(i % tiles_per_seq == 0)
    def _():
        hext_ref[0:halo, :] = jnp.zeros((halo, d), _F32)

    @pl.when(i % tiles_per_seq != 0)
    def _():
        hext_ref[0:halo, :] = hext_ref[tm:tm + halo, :]

    groups = range(0, tm, row_group)
    for g0 in groups:
        xb = x_ref[g0:g0 + row_group, :].astype(_BF16)
        hext_ref[halo + g0:halo + g0 + row_group, :] = (
            _dot(xb, win_ref[:, 0:d]) * jax.nn.sigmoid(_dot(xb, win_ref[:, d:2 * d])))
    width = cw_ref.shape[0]
    shift_rows = row_group + halo - SUBLANES
    for g0 in groups:
        for r in range(1, SUBLANES):
            shift_ref[r - 1, g0:g0 + shift_rows, :] = hext_ref[g0 + r:g0 + r + shift_rows, :]
        for lc in range(d // lane_chunk):
            cs = slice(lc * lane_chunk, (lc + 1) * lane_chunk)
            bias = cb_ref[:, cs]
            for r0 in range(g0, g0 + row_group, row_chunk):
                acc = jnp.zeros((row_chunk, lane_chunk), _F32) + bias
                for k in range(width):
                    q, r = divmod(halo - (width - 1) + k, SUBLANES)
                    off = q * SUBLANES + r0
                    if r == 0:
                        src = hext_ref[off:off + row_chunk, cs]
                    else:
                        src = shift_ref[r - 1, off:off + row_chunk, cs]
                    acc = acc + cw_ref[k:k + 1, cs] * src
                conv_ref[r0:r0 + row_chunk, cs] = acc
    for g0 in groups:
        rows = slice(g0, g0 + row_group)
        c = _layer_norm(conv_ref[rows, :], cg_ref[...], cbeta_ref[...])
        y = _dot(jax.nn.silu(c).astype(_BF16), wout_ref[...])
        _store_residual_ln(x_ref, y, rows, g_ref, b_ref, alpha, o_ref, op_ref)


def _mixer_d(x2, w_in, conv_w, conv_b, c_ln_g, c_ln_b, w_out, ln_g, ln_b, *, seq, alpha, tm=512, row_group=256):
    n, d = x2.shape
    tm = min(tm, seq)
    row_group = min(row_group, tm)
    halo = 32
    assert conv_w.shape[0] - 1 <= halo
    kern = functools.partial(_mixer_d_kernel, tm=tm, d=d, tiles_per_seq=seq // tm, alpha=alpha, halo=halo,
                             row_chunk=64, lane_chunk=LANES, row_group=row_group)
    return pl.pallas_call(
        kern,
        grid=(n // tm,),
        in_specs=[_rows(tm, d), _full(w_in.shape), _full(conv_w.shape), _full((1, d)), _full((1, d)),
                  _full((1, d)), _full(w_out.shape), _full((1, d)), _full((1, d))],
        out_specs=(_rows(tm, d), _rows(tm, d // 2)),
        out_shape=_mixer_out_shapes(n, d),
        scratch_shapes=[pltpu.VMEM((tm + halo, d), _F32),
                        pltpu.VMEM((SUBLANES - 1, tm + halo - SUBLANES, d), _F32),
                        pltpu.VMEM((tm, d), _F32)],
        compiler_params=_params(),
        name="mixer_d",
    )(x2, w_in.astype(_BF16), conv_w, conv_b.reshape(1, d), c_ln_g.reshape(1, d), c_ln_b.reshape(1, d),
      w_out.astype(_BF16), ln_g.reshape(1, d), ln_b.reshape(1, d))


def _router_kernel(x_ref, w2_ref, bias_ref, eidx_ref, gate_ref, rank_ref, cnt_ref,
                   carry_ref, gs_ref, masked_ref, graw_ref, *, tm, n_exp):
    i = pl.program_id(0)

    @pl.when(i == 0)
    def _():
        carry_ref[...] = jnp.zeros_like(carry_ref)

    x = x_ref[...]
    x_hi = x.astype(_BF16)
    x_lo = (x - x_hi.astype(_F32)).astype(_BF16)
    parts = (_dot(x_hi, w2_ref[...]) + _dot(x_lo, w2_ref[...])).T
    logits = parts[0:n_exp, :] + parts[n_exp:2 * n_exp, :]
    scores = jax.nn.sigmoid(logits)
    biased = scores + bias_ref[...]
    per = n_exp // N_EXPERT_GROUPS
    neg = -jnp.inf
    sub = lax.broadcasted_iota(jnp.int32, (per, tm), 0)
    for g in range(N_EXPERT_GROUPS):
        bg = biased[g * per:(g + 1) * per, :]
        m1 = jnp.max(bg, axis=0, keepdims=True)
        first = jnp.min(jnp.where(bg == m1, sub, per), axis=0, keepdims=True)
        m2 = jnp.max(jnp.where(sub == first, neg, bg), axis=0, keepdims=True)
        gs_ref[g:g + 1, :] = m1 + m2

    giota = lax.broadcasted_iota(jnp.int32, (N_EXPERT_GROUPS, tm), 0)
    cur = gs_ref[...]
    gsel = jnp.zeros((N_EXPERT_GROUPS, tm), jnp.bool_)
    for _ in range(TOPK_GROUPS):
        m = jnp.max(cur, axis=0, keepdims=True)
        gi = jnp.min(jnp.where(cur == m, giota, N_EXPERT_GROUPS), axis=0, keepdims=True)
        pick = giota == gi
        gsel = gsel | pick
        cur = jnp.where(pick, neg, cur)
    gs_ref[...] = gsel.astype(_F32)
    for g in range(N_EXPERT_GROUPS):
        keep = gs_ref[g:g + 1, :] > 0.5
        masked_ref[g * per:(g + 1) * per, :] = jnp.where(keep, biased[g * per:(g + 1) * per, :], neg)

    eiota = lax.broadcasted_iota(jnp.int32, (n_exp, tm), 0)
    cur = masked_ref[...]
    sel = jnp.zeros((n_exp, tm), jnp.bool_)
    for k in range(TOP_K):
        m = jnp.max(cur, axis=0, keepdims=True)
        ei = jnp.min(jnp.where(cur == m, eiota, n_exp), axis=0, keepdims=True)
        pick = eiota == ei
        eidx_ref[k:k + 1, :] = ei
        graw_ref[k:k + 1, :] = jnp.sum(jnp.where(pick, scores, 0.0), axis=0, keepdims=True)
        sel = sel | pick
        cur = jnp.where(pick, neg, cur)
    graw = graw_ref[...]
    gate_ref[...] = graw / jnp.sum(graw, axis=0, keepdims=True) * ROUTED_SCALE

    sel_b = sel.astype(_F32).astype(_BF16)
    before = (lax.broadcasted_iota(jnp.int32, (tm, tm), 0)
              < lax.broadcasted_iota(jnp.int32, (tm, tm), 1)).astype(_F32).astype(_BF16)
    rank_all = _dot(sel_b, before) + carry_ref[:, 0:1]
    for k in range(TOP_K):
        pick = eiota == eidx_ref[k:k + 1, :]
        rank_ref[k:k + 1, :] = jnp.sum(jnp.where(pick, rank_all, 0.0), axis=0, keepdims=True).astype(jnp.int32)
    carry_ref[...] = carry_ref[...] + jnp.sum(sel.astype(_F32), axis=1, keepdims=True)
    cnt_ref[...] = carry_ref[...]


def _router(x1, router_w, router_bias, *, tm=512):
    n, d = x1.shape
    n_exp = router_w.shape[1]
    tm = min(tm, n)
    kern = functools.partial(_router_kernel, tm=tm, n_exp=n_exp)
    tok = pl.BlockSpec((TOP_K, tm), lambda i: (0, i))
    w_hi = router_w.astype(_BF16)
    w2 = jnp.concatenate([w_hi, (router_w - w_hi.astype(_F32)).astype(_BF16)], axis=1)
    return pl.pallas_call(
        kern,
        grid=(n // tm,),
        in_specs=[_rows(tm, d), _full((d, 2 * n_exp)), _full((n_exp, 1))],
        out_specs=(tok, tok, tok, pl.BlockSpec((n_exp, LANES), lambda i: (0, 0))),
        out_shape=(jax.ShapeDtypeStruct((TOP_K, n), jnp.int32), jax.ShapeDtypeStruct((TOP_K, n), _F32),
                   jax.ShapeDtypeStruct((TOP_K, n), jnp.int32), jax.ShapeDtypeStruct((n_exp, LANES), _F32)),
        scratch_shapes=[pltpu.VMEM((n_exp, LANES), _F32), pltpu.VMEM((N_EXPERT_GROUPS, tm), _F32),
                        pltpu.VMEM((n_exp, tm), _F32), pltpu.VMEM((TOP_K, tm), _F32)],
        compiler_params=_params(),
        name="moe_router",
    )(x1, w2, router_bias.reshape(n_exp, 1))


def _tables_kernel(cnt_ref, eidx_ref, rank_ref, dest_ref, blk_ref, *, n_exp, n_blk_pad):
    counts = cnt_ref[...].astype(jnp.int32)
    shift = ROW_BLOCK.bit_length() - 1
    padded = lax.shift_left(lax.shift_right_logical(counts + (ROW_BLOCK - 1), shift), shift)
    lower = (lax.broadcasted_iota(jnp.int32, (n_exp, n_exp), 1)
             < lax.broadcasted_iota(jnp.int32, (n_exp, n_exp), 0)).astype(_F32)
    pad_start = jnp.dot(lower, padded.astype(_F32), precision=lax.Precision.HIGHEST,
                        preferred_element_type=_F32).astype(jnp.int32)
    pad_end = pad_start + padded
    eidx = eidx_ref[...]
    dest = rank_ref[...]
    for e in range(n_exp):
        dest = dest + jnp.where(eidx == e, pad_start[e:e + 1, 0:1], 0)
    dest_ref[...] = dest

    @pl.when(pl.program_id(0) == 0)
    def _():
        bstart = lax.broadcasted_iota(jnp.int32, (1, n_blk_pad), 1) * ROW_BLOCK
        blk_e = jnp.zeros((1, n_blk_pad), jnp.int32)
        for e in range(n_exp):
            blk_e = blk_e + (pad_end[e:e + 1, 0:1] <= bstart).astype(jnp.int32)
        blk_e = jnp.minimum(blk_e, n_exp - 1)
        used_end = pad_start + counts
        valid = jnp.zeros((1, n_blk_pad), jnp.int32)
        for e in range(n_exp):
            valid = valid + jnp.where(blk_e == e, used_end[e:e + 1, 0:1], 0)
        valid = jnp.clip(valid - bstart, 0, ROW_BLOCK)
        end_blk = lax.shift_right_logical(pad_end, shift)
        region_end = jnp.zeros((1, n_blk_pad), jnp.int32)
        for e in range(n_exp):
            region_end = region_end + jnp.where(blk_e == e, end_blk[e:e + 1, 0:1], 0)
        blk_ref[BLK_EXPERT:BLK_EXPERT + 1, :] = blk_e
        blk_ref[BLK_VALID:BLK_VALID + 1, :] = valid
        blk_ref[BLK_REGION_END:BLK_REGION_END + 1, :] = region_end
        blk_ref[BLK_N_ACTIVE:BLK_N_ACTIVE + 1, :] = jnp.broadcast_to(end_blk[n_exp - 1:n_exp, 0:1], (1, n_blk_pad))
        blk_ref[BLK_ROWS_USED:8, :] = jnp.zeros((8 - BLK_ROWS_USED, n_blk_pad), jnp.int32)


def _tables(cnt, eidx, rank, *, n_blk, tm=2048):
    n = eidx.shape[1]
    n_exp = cnt.shape[0]
    tm = min(tm, n)
    n_blk_pad = pl.cdiv(n_blk, LANES) * LANES
    kern = functools.partial(_tables_kernel, n_exp=n_exp, n_blk_pad=n_blk_pad)
    tok = pl.BlockSpec((TOP_K, tm), lambda i: (0, i))
    return pl.pallas_call(
        kern,
        grid=(n // tm,),
        in_specs=[_full((n_exp, LANES)), tok, tok],
        out_specs=(tok, pl.BlockSpec((8, n_blk_pad), lambda i: (0, 0))),
        out_shape=(jax.ShapeDtypeStruct((TOP_K, n), jnp.int32), jax.ShapeDtypeStruct((8, n_blk_pad), jnp.int32)),
        compiler_params=_params(),
        name="moe_tables",
    )(cnt, eidx, rank)


def _ffn_kernel(blk_ref, xg_hbm, wg_hbm, wu_hbm, wd_hbm, y_hbm,
                xbuf, ybuf, wg_f, wu_f, wd_f, wg_s, wu_s, wd_s, in_sem, out_sem, w_sem, *, half, layer):
    n_active = blk_ref[BLK_N_ACTIVE, 0]

    def in_copy(g, slot):
        return pltpu.make_async_copy(xg_hbm.at[pl.ds(g * ROW_BLOCK, ROW_BLOCK)], xbuf.at[slot], in_sem.at[slot])

    def out_copy(g, slot):
        return pltpu.make_async_copy(ybuf.at[slot], y_hbm.at[pl.ds(g * ROW_BLOCK, ROW_BLOCK)], out_sem.at[slot])

    def weight_copies(e):
        return (pltpu.make_async_copy(wg_hbm.at[layer, e], wg_f, w_sem.at[0]),
                pltpu.make_async_copy(wu_hbm.at[layer, e], wu_f, w_sem.at[1]),
                pltpu.make_async_copy(wd_hbm.at[layer, e], wd_f, w_sem.at[2]))

    def sub_blocks(slot, n_sub, valid):
        hidden = []
        for s in range(n_sub):
            rs = slice(s * FFN_SUB, (s + 1) * FFN_SUB)
            keep = lax.broadcasted_iota(jnp.int32, (FFN_SUB, 1), 0) + s * FFN_SUB < valid
            lo, hi = _unpack_pair(xbuf[slot, rs, :])
            lo = jnp.where(keep, lo, 0.0).astype(_BF16)
            hi = jnp.where(keep, hi, 0.0).astype(_BF16)
            g = _dot(lo, wg_s[0:half, :]) + _dot(hi, wg_s[half:2 * half, :])
            u = _dot(lo, wu_s[0:half, :]) + _dot(hi, wu_s[half:2 * half, :])
            hidden.append((jax.nn.silu(g) * u).astype(_BF16))
        for s, h in enumerate(hidden):
            rs = slice(s * FFN_SUB, (s + 1) * FFN_SUB)
            ybuf[slot, rs, :] = _pack_pair(_dot(h, wd_s[:, 0:half]), _dot(h, wd_s[:, half:2 * half]))

    def block(g, slot):
        e = blk_ref[BLK_EXPERT, g]
        valid = blk_ref[BLK_VALID, g]

        @pl.when(g + 1 < n_active)
        def _():
            in_copy(g + 1, 1 - slot).start()

        @pl.when((g == 0) | (e != blk_ref[BLK_EXPERT, jnp.maximum(g - 1, 0)]))
        def _():
            for c in weight_copies(e):
                c.wait()
            wg_s[...] = wg_f[...].astype(_BF16)
            wu_s[...] = wu_f[...].astype(_BF16)
            wd_s[...] = wd_f[...].astype(_BF16)
            nxt = blk_ref[BLK_REGION_END, g]

            @pl.when(nxt < n_active)
            def _():
                for c in weight_copies(blk_ref[BLK_EXPERT, nxt]):
                    c.start()

        in_copy(g, slot).wait()

        @pl.when(g >= 2)
        def _():
            out_copy(g - 2, slot).wait()

        @pl.when(valid > FFN_SUB)
        def _():
            sub_blocks(slot, ROW_BLOCK // FFN_SUB, valid)

        @pl.when(valid <= FFN_SUB)
        def _():
            sub_blocks(slot, 1, valid)

        out_copy(g, slot).start()

    in_copy(0, 0).start()
    for c in weight_copies(blk_ref[BLK_EXPERT, 0]):
        c.start()

    def pair(p, carry):
        g = 2 * p
        block(g, 0)

        @pl.when(g + 1 < n_active)
        def _():
            block(g + 1, 1)

        return carry

    lax.fori_loop(0, lax.shift_right_logical(n_active + 1, 1), pair, 0)

    last = n_active - 1
    for slot in range(2):
        @pl.when((last >= slot) & (lax.rem(last - slot, 2) == 0))
        def _():
            out_copy(last, slot).wait()

        @pl.when((last >= 1) & (lax.rem(last - 1 - slot, 2) == 0))
        def _():
            out_copy(last - 1, slot).wait()


def _expert_ffn(blk, xg, w_gate, w_up, w_down, *, n_blk, layer):
    _, n_exp, d, hid = w_gate.shape
    half = d // 2
    kern = functools.partial(_ffn_kernel, half=half, layer=layer)
    any_spec = pl.BlockSpec(memory_space=pl.ANY)
    grid_spec = pltpu.PrefetchScalarGridSpec(
        num_scalar_prefetch=1,
        grid=(1,),
        in_specs=[any_spec, any_spec, any_spec, any_spec],
        out_specs=any_spec,
        scratch_shapes=[pltpu.VMEM((2, ROW_BLOCK, half), jnp.uint32), pltpu.VMEM((2, ROW_BLOCK, half), jnp.uint32),
                        pltpu.VMEM((d, hid), _F32), pltpu.VMEM((d, hid), _F32), pltpu.VMEM((hid, d), _F32),
                        pltpu.VMEM((d, hid), _BF16), pltpu.VMEM((d, hid), _BF16), pltpu.VMEM((hid, d), _BF16),
                        pltpu.SemaphoreType.DMA((2,)), pltpu.SemaphoreType.DMA((2,)), pltpu.SemaphoreType.DMA((3,))],
    )
    return pl.pallas_call(
        kern,
        grid_spec=grid_spec,
        out_shape=jax.ShapeDtypeStruct((n_blk * ROW_BLOCK, half), jnp.uint32),
        compiler_params=_params(),
        name="moe_expert_ffn",
    )(blk, xg, w_gate, w_up, w_down)


def _combine_kernel(x_ref, yg_ref, gate_ref, sg_ref, su_ref, sd_ref, g_ref, b_ref, o_ref, gpad_ref, *,
                    tm, d, alpha):
    x = x_ref[...]
    xb = x.astype(_BF16)
    half = d // 2
    gpad_ref[...] = jnp.zeros_like(gpad_ref)
    gpad_ref[0:TOP_K, :] = gate_ref[...]
    gate_t = gpad_ref[...].T
    lo_acc = jnp.zeros((tm, half), _F32)
    hi_acc = jnp.zeros((tm, half), _F32)
    for k in range(TOP_K):
        lo, hi = _unpack_pair(yg_ref[k])
        w = gate_t[:, k:k + 1]
        lo_acc = lo_acc + w * lo
        hi_acc = hi_acc + w * hi
    hs = (jax.nn.silu(_dot(xb, sg_ref[...])) * _dot(xb, su_ref[...])).astype(_BF16)
    shared = _dot(hs, sd_ref[...])
    f = jnp.concatenate([lo_acc, hi_acc], axis=1) + shared
    o_ref[...] = _layer_norm(alpha * x + f, g_ref[...], b_ref[...])


def _combine(x1, yg, gate, sh_gate, sh_up, sh_down, ln_g, ln_b, *, alpha, tm=512):
    n, d = x1.shape
    tm = min(tm, n)
    hid = sh_gate.shape[1]
    kern = functools.partial(_combine_kernel, tm=tm, d=d, alpha=alpha)
    return pl.pallas_call(
        kern,
        grid=(n // tm,),
        in_specs=[_rows(tm, d), pl.BlockSpec((TOP_K, tm, d // 2), lambda i: (0, i, 0)),
                  pl.BlockSpec((TOP_K, tm), lambda i: (0, i)), _full((d, hid)), _full((d, hid)),
                  _full((hid, d)), _full((1, d)), _full((1, d))],
        out_specs=_rows(tm, d),
        out_shape=jax.ShapeDtypeStruct((n, d), _F32),
        scratch_shapes=[pltpu.VMEM((LANES, tm), _F32)],
        compiler_params=_params(),
        name="moe_combine",
    )(x1, yg, gate, sh_gate.astype(_BF16), sh_up.astype(_BF16), sh_down.astype(_BF16),
      ln_g.reshape(1, d), ln_b.reshape(1, d))


SC_CORES = 2
SC_SUBCORES = 16
SC_WORKERS = SC_CORES * SC_SUBCORES
SC_WINDOW = 64


def _sc_mesh():
    return plsc.VectorSubcoreMesh(core_axis_name="c", subcore_axis_name="s",
                                  num_cores=SC_CORES, num_subcores=SC_SUBCORES)


def _sc_worker_id():
    return lax.axis_index("s") * SC_CORES + lax.axis_index("c")


def _dispatch_rows(xp, dest, n_rows):
    n, width = xp.shape
    n_slots = dest.shape[0]
    n_win = n // SC_WINDOW
    per_worker = n_win // SC_WORKERS
    assert per_worker * SC_WORKERS * SC_WINDOW == n and per_worker % 2 == 0
    dest_w = dest.reshape(n_slots, n_win, SC_WINDOW).transpose(1, 0, 2)

    def body(x_hbm, d_hbm, out_hbm, idx_v, rows_v, row_sem, idx_sem, out_sem):
        first = _sc_worker_id() * per_worker

        def load_rows(j, b):
            return pltpu.make_async_copy(x_hbm.at[pl.ds((first + j) * SC_WINDOW, SC_WINDOW)], rows_v.at[b],
                                         row_sem.at[b])

        def load_idx(j, b):
            return pltpu.make_async_copy(d_hbm.at[first + j], idx_v.at[b], idx_sem.at[b])

        def scatter(b, k):
            return pltpu.make_async_copy(rows_v.at[b], out_hbm.at[idx_v.at[b].at[k]], out_sem.at[b])

        for b in range(2):
            load_rows(b, b).start()
            load_idx(b, b).start()

        @pl.loop(0, per_worker, step=2)
        def _(j0):
            for b in range(2):
                j = j0 + b
                load_rows(j, b).wait()
                load_idx(j, b).wait()
                for k in range(n_slots):
                    scatter(b, k).start()
                for k in range(n_slots):
                    scatter(b, k).wait()

                @pl.when(j + 2 < per_worker)
                def _():
                    load_rows(j + 2, b).start()
                    load_idx(j + 2, b).start()

    return pl.kernel(
        body,
        out_type=jax.ShapeDtypeStruct((n_rows, width), xp.dtype),
        mesh=_sc_mesh(),
        scratch_types=[pltpu.VMEM((2, n_slots, SC_WINDOW), jnp.int32), pltpu.VMEM((2, SC_WINDOW, width), xp.dtype),
                       pltpu.SemaphoreType.DMA((2,)), pltpu.SemaphoreType.DMA((2,)), pltpu.SemaphoreType.DMA((2,))],
        name="moe_dispatch_rows",
    )(xp, dest_w)


def _gather_rows(yp, dest):
    n_slots, n = dest.shape
    width = yp.shape[1]
    m = n_slots * n
    per_worker = m // SC_WORKERS
    n_win = per_worker // SC_WINDOW
    assert n_win * SC_WINDOW * SC_WORKERS == m and n_win % 2 == 0

    def body(y_hbm, idx_hbm, out_hbm, idx_v, rows_v, in_sem, out_sem):
        base = _sc_worker_id() * per_worker
        pltpu.sync_copy(idx_hbm.at[pl.ds(base, per_worker)], idx_v)

        def gather(w, b):
            return pltpu.make_async_copy(y_hbm.at[idx_v.at[pl.ds(w * SC_WINDOW, SC_WINDOW)]], rows_v.at[b],
                                         in_sem.at[b])

        def put(w, b):
            return pltpu.make_async_copy(rows_v.at[b], out_hbm.at[pl.ds(base + w * SC_WINDOW, SC_WINDOW)],
                                         out_sem.at[b])

        for b in range(2):
            gather(b, b).start()

        @pl.loop(0, n_win, step=2)
        def _(w0):
            for b in range(2):
                w = w0 + b
                gather(w, b).wait()
                put(w, b).start()

                @pl.when(w >= 1)
                def _():
                    put(w - 1, 1 - b).wait()

                    @pl.when(w + 1 < n_win)
                    def _():
                        gather(w + 1, 1 - b).start()

        put(n_win - 1, 1).wait()

    out = pl.kernel(
        body,
        out_type=jax.ShapeDtypeStruct((m, width), yp.dtype),
        mesh=_sc_mesh(),
        scratch_types=[pltpu.VMEM((per_worker,), jnp.int32), pltpu.VMEM((2, SC_WINDOW, width), yp.dtype),
                       pltpu.SemaphoreType.DMA((2,)), pltpu.SemaphoreType.DMA((2,))],
        name="moe_gather_rows",
    )(yp, dest.reshape(m))
    return out.reshape(n_slots, n, width)


def _moe(x1, x1p, router_w, router_bias, w_gate, w_up, w_down, sh_gate, sh_up, sh_down, ln_g, ln_b, *, alpha,
         layer):
    n, d = x1.shape
    n_exp = router_w.shape[1]
    n_blk = (n * TOP_K + n_exp * (ROW_BLOCK - 1) + ROW_BLOCK - 1) // ROW_BLOCK
    eidx, gate, rank, cnt = _router(x1, router_w, router_bias)
    dest, blk = _tables(cnt, eidx, rank, n_blk=n_blk)
    xg = _dispatch_rows(x1p, dest, n_blk * ROW_BLOCK)
    y = _expert_ffn(blk, xg, w_gate, w_up, w_down, n_blk=n_blk, layer=layer)
    yg = _gather_rows(y, dest)
    return _combine(x1, yg, gate, sh_gate, sh_up, sh_down, ln_g, ln_b, alpha=alpha)


def kernel(x, a_w_in, a_conv_w, a_w_out, b_w_in, b_v_ln_g, b_v_ln_b, b_w_s, b_s_bias, b_w_out, c_w_in, c_w_grp,
           c_scale, c_w_out, d_w_in, d_conv_w, d_conv_b, d_ln_g, d_ln_b, d_w_out, ln1_g, ln1_b, ln2_g, ln2_b,
           router_w, router_bias, exp_w_gate, exp_w_up, exp_w_down, sh_w_gate, sh_w_up, sh_w_down):
    bsz, seq, d = x.shape
    depth = ln1_g.shape[0]
    n_mixers = 4
    alpha = (2 * depth) ** 0.25
    h = x.reshape(bsz * seq, d)
    for i in range(depth):
        m, j = i % n_mixers, i // n_mixers
        if m == 0:
            x1, x1p = _mixer_a(h, a_w_in[j], a_conv_w[j], a_w_out[j], ln1_g[i], ln1_b[i], seq=seq, alpha=alpha)
        elif m == 1:
            x1, x1p = _mixer_b(h, b_w_in[j], b_v_ln_g[j], b_v_ln_b[j], b_w_s[j], b_s_bias[j], b_w_out[j],
                               ln1_g[i], ln1_b[i], alpha=alpha)
        elif m == 2:
            x1, x1p = _mixer_c(h, c_w_in[j], c_w_grp[j], c_scale[j], c_w_out[j], ln1_g[i], ln1_b[i],
                               seq=seq, alpha=alpha)
        else:
            x1, x1p = _mixer_d(h, d_w_in[j], d_conv_w[j], d_conv_b[j], d_ln_g[j], d_ln_b[j], d_w_out[j],
                               ln1_g[i], ln1_b[i], seq=seq, alpha=alpha)
        h = _moe(x1, x1p, router_w[i], router_bias[i], exp_w_gate, exp_w_up, exp_w_down,
                 sh_w_gate[i], sh_w_up[i], sh_w_down[i], ln2_g[i], ln2_b[i], alpha=alpha, layer=i)
    return h.reshape(bsz, seq, d)
```

```python
import functools

import jax
import jax.numpy as jnp
from jax import lax
from jax.experimental import pallas as pl
from jax.experimental.pallas import tpu as pltpu
from jax.experimental.pallas import tpu_sc as plsc

LN_EPS = 1e-5
CHUNK = 128
GMLP_HEADS = 8
POOL_WINDOWS = (2, 4, 8, 16)
N_EXPERT_GROUPS = 8
TOPK_GROUPS = 4
TOP_K = 8
ROUTED_SCALE = 2.5
ROW_BLOCK = 512
FFN_SUB = 256
FFN_SLOTS = 4
BLK_EXPERT, BLK_VALID, BLK_REGION_END, BLK_N_ACTIVE, BLK_ROWS_USED = 0, 1, 2, 3, 4
LANES = 128
SUBLANES = 8
VMEM_LIMIT = 56 * 1024 * 1024

_BF16 = jnp.bfloat16
_F32 = jnp.float32


def _dot(a, b):
    return jnp.dot(a, b, preferred_element_type=_F32)


def _layer_norm(z, g, b):
    mu = jnp.mean(z, axis=-1, keepdims=True)
    zc = z - mu
    var = jnp.mean(zc * zc, axis=-1, keepdims=True)
    return zc * lax.rsqrt(var + LN_EPS) * g + b


def _pack_pair(lo, hi):
    lo_bits = lax.bitcast_convert_type(lo.astype(_BF16).astype(_F32), jnp.uint32)
    hi_bits = lax.bitcast_convert_type(hi.astype(_BF16).astype(_F32), jnp.uint32)
    return lax.shift_right_logical(lo_bits, jnp.uint32(16)) | (hi_bits & jnp.uint32(0xFFFF0000))


def _unpack_pair(p):
    lo = lax.bitcast_convert_type(lax.shift_left(p, jnp.uint32(16)), _F32)
    hi = lax.bitcast_convert_type(p & jnp.uint32(0xFFFF0000), _F32)
    return lo, hi


def _store_residual_ln(x_ref, y, rows, g_ref, b_ref, alpha, o_ref, op_ref):
    x1 = _layer_norm(alpha * x_ref[rows, :] + y, g_ref[...], b_ref[...])
    o_ref[rows, :] = x1
    half = x1.shape[-1] // 2
    op_ref[rows, :] = _pack_pair(x1[:, :half], x1[:, half:])


def _params(n_axes=1):
    return pltpu.CompilerParams(dimension_semantics=("arbitrary",) * n_axes, vmem_limit_bytes=VMEM_LIMIT)


def _full(shape):
    nd = len(shape)
    return pl.BlockSpec(shape, lambda i, _nd=nd: (0,) * _nd, pipeline_mode=pl.Buffered(1))


def _rows(tm, width):
    return pl.BlockSpec((tm, width), lambda i: (i, 0))


def _mixer_out_shapes(n, d):
    return (jax.ShapeDtypeStruct((n, d), _F32), jax.ShapeDtypeStruct((n, d // 2), jnp.uint32))


def _mixer_a_kernel(x_ref, win_ref, cw_ref, wout_ref, g_ref, b_ref, o_ref, op_ref, zext_ref, *,
                    tm, d, tiles_per_seq, alpha, halo, row_group):
    i = pl.program_id(0)

    @pl.when(i % tiles_per_seq == 0)
    def _():
        zext_ref[0:halo, :] = jnp.zeros((halo, d), _F32)

    @pl.when(i % tiles_per_seq != 0)
    def _():
        zext_ref[0:halo, :] = zext_ref[tm:tm + halo, :]

    groups = range(0, tm, row_group)
    gates = []
    for g0 in groups:
        xb = x_ref[g0:g0 + row_group, :].astype(_BF16)
        gates.append(_dot(xb, win_ref[:, 0:d]))
        zext_ref[halo + g0:halo + g0 + row_group, :] = (
            _dot(xb, win_ref[:, d:2 * d]) * _dot(xb, win_ref[:, 2 * d:3 * d]))
    width = cw_ref.shape[0]
    for g0, b_gate in zip(groups, gates):
        conv = jnp.zeros((row_group, d), _F32)
        for k in range(width):
            off = halo - (width - 1) + k + g0
            conv = conv + cw_ref[k:k + 1, :] * zext_ref[off:off + row_group, :]
        y = _dot((b_gate * conv).astype(_BF16), wout_ref[...])
        rows = slice(g0, g0 + row_group)
        _store_residual_ln(x_ref, y, rows, g_ref, b_ref, alpha, o_ref, op_ref)


def _mixer_a(x2, w_in, conv_w, w_out, ln_g, ln_b, *, seq, alpha, tm=512, row_group=256):
    n, d = x2.shape
    tm = min(tm, seq)
    row_group = min(row_group, tm)
    halo = 8
    assert conv_w.shape[0] - 1 <= halo
    kern = functools.partial(_mixer_a_kernel, tm=tm, d=d, tiles_per_seq=seq // tm, alpha=alpha, halo=halo,
                             row_group=row_group)
    return pl.pallas_call(
        kern,
        grid=(n // tm,),
        in_specs=[_rows(tm, d), _full(w_in.shape), _full(conv_w.shape), _full(w_out.shape),
                  _full((1, d)), _full((1, d))],
        out_specs=(_rows(tm, d), _rows(tm, d // 2)),
        out_shape=_mixer_out_shapes(n, d),
        scratch_shapes=[pltpu.VMEM((tm + halo, d), _F32)],
        compiler_params=_params(),
        name="mixer_a",
    )(x2, w_in.astype(_BF16), conv_w, w_out.astype(_BF16), ln_g.reshape(1, d), ln_b.reshape(1, d))


def _mixer_b_kernel(x_ref, win_ref, vg_ref, vb_ref, ws_ref, sbt_ref, wout_ref, g_ref, b_ref, o_ref, op_ref,
                    gated_ref, *, tm, half, heads, alpha, row_group):
    groups = [slice(r, r + row_group) for r in range(0, tm, row_group)]
    us, vns = [], []
    for rows in groups:
        xb = x_ref[rows, :].astype(_BF16)
        us.append(jax.nn.gelu(_dot(xb, win_ref[:, 0:half])))
        v = jax.nn.gelu(_dot(xb, win_ref[:, half:2 * half]))
        vns.append(_layer_norm(v, vg_ref[...], vb_ref[...]).astype(_BF16))
    hd = half // heads
    row = lax.broadcasted_iota(jnp.int32, (CHUNK, CHUNK), 0)
    col = lax.broadcasted_iota(jnp.int32, (CHUNK, CHUNK), 1)
    causal = row >= col
    for h in range(heads):
        w_c = jnp.where(causal, ws_ref[h], 0.0).astype(_BF16)
        bias = sbt_ref[:, h:h + 1]
        cs = slice(h * hd, (h + 1) * hd)
        for rows, u, vn in zip(groups, us, vns):
            for c in range(row_group // CHUNK):
                rs = slice(c * CHUNK, (c + 1) * CHUNK)
                s = _dot(w_c, vn[rs, cs]) + bias
                gated_ref[rows.start + c * CHUNK:rows.start + (c + 1) * CHUNK, cs] = (u[rs, cs] * s).astype(_BF16)
    for rows in groups:
        y = _dot(gated_ref[rows, :], wout_ref[...])
        _store_residual_ln(x_ref, y, rows, g_ref, b_ref, alpha, o_ref, op_ref)


def _mixer_b(x2, w_in, v_ln_g, v_ln_b, w_s, s_bias, w_out, ln_g, ln_b, *, alpha, tm=512, row_group=256):
    n, d = x2.shape
    half = w_in.shape[1] // 2
    heads = w_s.shape[0]
    kern = functools.partial(_mixer_b_kernel, tm=tm, half=half, heads=heads, alpha=alpha, row_group=row_group)
    return pl.pallas_call(
        kern,
        grid=(n // tm,),
        in_specs=[_rows(tm, d), _full(w_in.shape), _full((1, half)), _full((1, half)), _full(w_s.shape),
                  _full((CHUNK, heads)), _full(w_out.shape), _full((1, d)), _full((1, d))],
        out_specs=(_rows(tm, d), _rows(tm, d // 2)),
        out_shape=_mixer_out_shapes(n, d),
        scratch_shapes=[pltpu.VMEM((tm, half), _BF16)],
        compiler_params=_params(),
        name="mixer_b",
    )(x2, w_in.astype(_BF16), v_ln_g.reshape(1, half), v_ln_b.reshape(1, half), w_s, s_bias.T,
      w_out.astype(_BF16), ln_g.reshape(1, d), ln_b.reshape(1, d))


def _mixer_c_kernel(x_ref, win_ref, wgrp_ref, scale_ref, wout_ref, g_ref, b_ref, o_ref, op_ref, hext_ref, *,
                    tm, d, tiles_per_seq, alpha, halo, row_group):
    i = pl.program_id(0)

    @pl.when(i % tiles_per_seq == 0)
    def _():
        hext_ref[0:halo, :] = jnp.zeros((halo, d), _F32)

    @pl.when(i % tiles_per_seq != 0)
    def _():
        hext_ref[0:halo, :] = hext_ref[tm:tm + halo, :]

    groups = range(0, tm, row_group)
    for g0 in groups:
        hext_ref[halo + g0:halo + g0 + row_group, :] = _dot(x_ref[g0:g0 + row_group, :].astype(_BF16), win_ref[...])
    gw = d // len(POOL_WINDOWS)
    for g0 in groups:
        pos = (i % tiles_per_seq) * tm + g0 + lax.broadcasted_iota(jnp.int32, (row_group, 1), 0) + 1
        y = jnp.zeros((row_group, d), _F32)
        for g, win in enumerate(POOL_WINDOWS):
            cs = slice(g * gw, (g + 1) * gw)
            h = hext_ref[halo + g0:halo + g0 + row_group, cs]
            acc = h
            for j in range(1, win):
                acc = acc + hext_ref[halo + g0 - j:halo + g0 - j + row_group, cs]
            mean = acc / jnp.minimum(pos, win).astype(_F32)
            p = mean - h
            yg = _dot(p.astype(_BF16), wgrp_ref[g]) * scale_ref[:, cs]
            y = y + _dot(yg.astype(_BF16), wout_ref[cs, :])
        rows = slice(g0, g0 + row_group)
        _store_residual_ln(x_ref, y, rows, g_ref, b_ref, alpha, o_ref, op_ref)


def _mixer_c(x2, w_in, w_grp, scale, w_out, ln_g, ln_b, *, seq, alpha, tm=512, row_group=256):
    n, d = x2.shape
    tm = min(tm, seq)
    row_group = min(row_group, tm)
    halo = 16
    assert max(POOL_WINDOWS) <= halo
    kern = functools.partial(_mixer_c_kernel, tm=tm, d=d, tiles_per_seq=seq // tm, alpha=alpha, halo=halo,
                             row_group=row_group)
    return pl.pallas_call(
        kern,
        grid=(n // tm,),
        in_specs=[_rows(tm, d), _full(w_in.shape), _full(w_grp.shape), _full((1, d)), _full(w_out.shape),
                  _full((1, d)), _full((1, d))],
        out_specs=(_rows(tm, d), _rows(tm, d // 2)),
        out_shape=_mixer_out_shapes(n, d),
        scratch_shapes=[pltpu.VMEM((tm + halo, d), _F32)],
        compiler_params=_params(),
        name="mixer_c",
    )(x2, w_in.astype(_BF16), w_grp.astype(_BF16), scale.reshape(1, d), w_out.astype(_BF16),
      ln_g.reshape(1, d), ln_b.reshape(1, d))


def _mixer_d_kernel(x_ref, win_ref, cw_ref, cb_ref, cg_ref, cbeta_ref, wout_ref, g_ref, b_ref, o_ref, op_ref,
                    hext_ref, shift_ref, conv_ref, *, tm, d, tiles_per_seq, alpha, halo, row_chunk, lane_chunk,
                    row_group):
    i = pl.program_id(0)

    @pl.when(i % tiles_per_seq == 0)
    def _():
        hext_ref[0:halo, :] = jnp.zeros((halo, d), _F32)

    @pl.when(i % tiles_per_seq != 0)
    def _():
        hext_ref[0:halo, :] = hext_ref[tm:tm + halo, :]

    groups = range(0, tm, row_group)
    for g0 in groups:
        xb = x_ref[g0:g0 + row_group, :].astype(_BF16)
        hext_ref[halo + g0:halo + g0 + row_group, :] = (
            _dot(xb, win_ref[:, 0:d]) * jax.nn.sigmoid(_dot(xb, win_ref[:, d:2 * d])))
    width = cw_ref.shape[0]
    shift_rows = row_group + halo - SUBLANES
    for g0 in groups:
        for r in range(1, SUBLANES):
            shift_ref[r - 1, g0:g0 + shift_rows, :] = hext_ref[g0 + r:g0 + r + shift_rows, :]
        for lc in range(d // lane_chunk):
            cs = slice(lc * lane_chunk, (lc + 1) * lane_chunk)
            bias = cb_ref[:, cs]
            for r0 in range(g0, g0 + row_group, row_chunk):
                acc = jnp.zeros((row_chunk, lane_chunk), _F32) + bias
                for k in range(width):
                    q, r = divmod(halo - (width - 1) + k, SUBLANES)
                    off = q * SUBLANES + r0
                    if r == 0:
                        src = hext_ref[off:off + row_chunk, cs]
                    else:
                        src = shift_ref[r - 1, off:off + row_chunk, cs]
                    acc = acc + cw_ref[k:k + 1, cs] * src
                conv_ref[r0:r0 + row_chunk, cs] = acc
    for g0 in groups:
        rows = slice(g0, g0 + row_group)
        c = _layer_norm(conv_ref[rows, :], cg_ref[...], cbeta_ref[...])
        y = _dot(jax.nn.silu(c).astype(_BF16), wout_ref[...])
        _store_residual_ln(x_ref, y, rows, g_ref, b_ref, alpha, o_ref, op_ref)


def _mixer_d(x2, w_in, conv_w, conv_b, c_ln_g, c_ln_b, w_out, ln_g, ln_b, *, seq, alpha, tm=512, row_group=256):
    n, d = x2.shape
    tm = min(tm, seq)
    row_group = min(row_group, tm)
    halo = 32
    assert conv_w.shape[0] - 1 <= halo
    kern = functools.partial(_mixer_d_kernel, tm=tm, d=d, tiles_per_seq=seq // tm, alpha=alpha, halo=halo,
                             row_chunk=64, lane_chunk=LANES, row_group=row_group)
    return pl.pallas_call(
        kern,
        grid=(n // tm,),
        in_specs=[_rows(tm, d), _full(w_in.shape), _full(conv_w.shape), _full((1, d)), _full((1, d)),
                  _full((1, d)), _full(w_out.shape), _full((1, d)), _full((1, d))],
        out_specs=(_rows(tm, d), _rows(tm, d // 2)),
        out_shape=_mixer_out_shapes(n, d),
        scratch_shapes=[pltpu.VMEM((tm + halo, d), _F32),
                        pltpu.VMEM((SUBLANES - 1, tm + halo - SUBLANES, d), _F32),
                        pltpu.VMEM((tm, d), _F32)],
        compiler_params=_params(),
        name="mixer_d",
    )(x2, w_in.astype(_BF16), conv_w, conv_b.reshape(1, d), c_ln_g.reshape(1, d), c_ln_b.reshape(1, d),
      w_out.astype(_BF16), ln_g.reshape(1, d), ln_b.reshape(1, d))


def _router_kernel(x_ref, w2_ref, bias_ref, eidx_ref, gate_ref, rank_ref, cnt_ref,
                   carry_ref, gs_ref, masked_ref, graw_ref, *, tm, n_exp):
    i = pl.program_id(0)

    @pl.when(i == 0)
    def _():
        carry_ref[...] = jnp.zeros_like(carry_ref)

    x = x_ref[...]
    x_hi = x.astype(_BF16)
    x_lo = (x - x_hi.astype(_F32)).astype(_BF16)
    parts = (_dot(x_hi, w2_ref[...]) + _dot(x_lo, w2_ref[...])).T
    logits = parts[0:n_exp, :] + parts[n_exp:2 * n_exp, :]
    scores = jax.nn.sigmoid(logits)
    biased = scores + bias_ref[...]
    per = n_exp // N_EXPERT_GROUPS
    neg = -jnp.inf
    sub = lax.broadcasted_iota(jnp.int32, (per, tm), 0)
    for g in range(N_EXPERT_GROUPS):
        bg = biased[g * per:(g + 1) * per, :]
        m1 = jnp.max(bg, axis=0, keepdims=True)
        first = jnp.min(jnp.where(bg == m1, sub, per), axis=0, keepdims=True)
        m2 = jnp.max(jnp.where(sub == first, neg, bg), axis=0, keepdims=True)
        gs_ref[g:g + 1, :] = m1 + m2

    giota = lax.broadcasted_iota(jnp.int32, (N_EXPERT_GROUPS, tm), 0)
    cur = gs_ref[...]
    gsel = jnp.zeros((N_EXPERT_GROUPS, tm), jnp.bool_)
    for _ in range(TOPK_GROUPS):
        m = jnp.max(cur, axis=0, keepdims=True)
        gi = jnp.min(jnp.where(cur == m, giota, N_EXPERT_GROUPS), axis=0, keepdims=True)
        pick = giota == gi
        gsel = gsel | pick
        cur = jnp.where(pick, neg, cur)
    gs_ref[...] = gsel.astype(_F32)
    for g in range(N_EXPERT_GROUPS):
        keep = gs_ref[g:g + 1, :] > 0.5
        masked_ref[g * per:(g + 1) * per, :] = jnp.where(keep, biased[g * per:(g + 1) * per, :], neg)

    eiota = lax.broadcasted_iota(jnp.int32, (n_exp, tm), 0)
    cur = masked_ref[...]
    sel = jnp.zeros((n_exp, tm), jnp.bool_)
    for k in range(TOP_K):
        m = jnp.max(cur, axis=0, keepdims=True)
        ei = jnp.min(jnp.where(cur == m, eiota, n_exp), axis=0, keepdims=True)
        pick = eiota == ei
        eidx_ref[k:k + 1, :] = ei
        graw_ref[k:k + 1, :] = jnp.sum(jnp.where(pick, scores, 0.0), axis=0, keepdims=True)
        sel = sel | pick
        cur = jnp.where(pick, neg, cur)
    graw = graw_ref[...]
    gate_ref[...] = graw / jnp.sum(graw, axis=0, keepdims=True) * ROUTED_SCALE

    sel_b = sel.astype(_F32).astype(_BF16)
    before = (lax.broadcasted_iota(jnp.int32, (tm, tm), 0)
              < lax.broadcasted_iota(jnp.int32, (tm, tm), 1)).astype(_F32).astype(_BF16)
    rank_all = _dot(sel_b, before) + carry_ref[:, 0:1]
    for k in range(TOP_K):
        pick = eiota == eidx_ref[k:k + 1, :]
        rank_ref[k:k + 1, :] = jnp.sum(jnp.where(pick, rank_all, 0.0), axis=0, keepdims=True).astype(jnp.int32)
    carry_ref[...] = carry_ref[...] + jnp.sum(sel.astype(_F32), axis=1, keepdims=True)
    cnt_ref[...] = carry_ref[...]


def _router(x1, router_w, router_bias, *, tm=512):
    n, d = x1.shape
    n_exp = router_w.shape[1]
    tm = min(tm, n)
    kern = functools.partial(_router_kernel, tm=tm, n_exp=n_exp)
    tok = pl.BlockSpec((TOP_K, tm), lambda i: (0, i))
    w_hi = router_w.astype(_BF16)
    w2 = jnp.concatenate([w_hi, (router_w - w_hi.astype(_F32)).astype(_BF16)], axis=1)
    return pl.pallas_call(
        kern,
        grid=(n // tm,),
        in_specs=[_rows(tm, d), _full((d, 2 * n_exp)), _full((n_exp, 1))],
        out_specs=(tok, tok, tok, pl.BlockSpec((n_exp, LANES), lambda i: (0, 0))),
        out_shape=(jax.ShapeDtypeStruct((TOP_K, n), jnp.int32), jax.ShapeDtypeStruct((TOP_K, n), _F32),
                   jax.ShapeDtypeStruct((TOP_K, n), jnp.int32), jax.ShapeDtypeStruct((n_exp, LANES), _F32)),
        scratch_shapes=[pltpu.VMEM((n_exp, LANES), _F32), pltpu.VMEM((N_EXPERT_GROUPS, tm), _F32),
                        pltpu.VMEM((n_exp, tm), _F32), pltpu.VMEM((TOP_K, tm), _F32)],
        compiler_params=_params(),
        name="moe_router",
    )(x1, w2, router_bias.reshape(n_exp, 1))


def _tables_kernel(cnt_ref, eidx_ref, rank_ref, dest_ref, blk_ref, *, n_exp, n_blk_pad):
    counts = cnt_ref[...].astype(jnp.int32)
    shift = ROW_BLOCK.bit_length() - 1
    padded = lax.shift_left(lax.shift_right_logical(counts + (ROW_BLOCK - 1), shift), shift)
    lower = (lax.broadcasted_iota(jnp.int32, (n_exp, n_exp), 1)
             < lax.broadcasted_iota(jnp.int32, (n_exp, n_exp), 0)).astype(_F32)
    pad_start = jnp.dot(lower, padded.astype(_F32), precision=lax.Precision.HIGHEST,
                        preferred_element_type=_F32).astype(jnp.int32)
    pad_end = pad_start + padded
    eidx = eidx_ref[...]
    dest = rank_ref[...]
    for e in range(n_exp):
        dest = dest + jnp.where(eidx == e, pad_start[e:e + 1, 0:1], 0)
    dest_ref[...] = dest

    @pl.when(pl.program_id(0) == 0)
    def _():
        bstart = lax.broadcasted_iota(jnp.int32, (1, n_blk_pad), 1) * ROW_BLOCK
        blk_e = jnp.zeros((1, n_blk_pad), jnp.int32)
        for e in range(n_exp):
            blk_e = blk_e + (pad_end[e:e + 1, 0:1] <= bstart).astype(jnp.int32)
        blk_e = jnp.minimum(blk_e, n_exp - 1)
        used_end = pad_start + counts
        valid = jnp.zeros((1, n_blk_pad), jnp.int32)
        for e in range(n_exp):
            valid = valid + jnp.where(blk_e == e, used_end[e:e + 1, 0:1], 0)
        valid = jnp.clip(valid - bstart, 0, ROW_BLOCK)
        end_blk = lax.shift_right_logical(pad_end, shift)
        region_end = jnp.zeros((1, n_blk_pad), jnp.int32)
        for e in range(n_exp):
            region_end = region_end + jnp.where(blk_e == e, end_blk[e:e + 1, 0:1], 0)
        blk_ref[BLK_EXPERT:BLK_EXPERT + 1, :] = blk_e
        blk_ref[BLK_VALID:BLK_VALID + 1, :] = valid
        blk_ref[BLK_REGION_END:BLK_REGION_END + 1, :] = region_end
        blk_ref[BLK_N_ACTIVE:BLK_N_ACTIVE + 1, :] = jnp.broadcast_to(end_blk[n_exp - 1:n_exp, 0:1], (1, n_blk_pad))
        blk_ref[BLK_ROWS_USED:8, :] = jnp.zeros((8 - BLK_ROWS_USED, n_blk_pad), jnp.int32)


def _tables(cnt, eidx, rank, *, n_blk, tm=2048):
    n = eidx.shape[1]
    n_exp = cnt.shape[0]
    tm = min(tm, n)
    n_blk_pad = pl.cdiv(n_blk, LANES) * LANES
    kern = functools.partial(_tables_kernel, n_exp=n_exp, n_blk_pad=n_blk_pad)
    tok = pl.BlockSpec((TOP_K, tm), lambda i: (0, i))
    return pl.pallas_call(
        kern,
        grid=(n // tm,),
        in_specs=[_full((n_exp, LANES)), tok, tok],
        out_specs=(tok, pl.BlockSpec((8, n_blk_pad), lambda i: (0, 0))),
        out_shape=(jax.ShapeDtypeStruct((TOP_K, n), jnp.int32), jax.ShapeDtypeStruct((8, n_blk_pad), jnp.int32)),
        compiler_params=_params(),
        name="moe_tables",
    )(cnt, eidx, rank)


def _ffn_kernel(blk_ref, xg_hbm, wg_hbm, wu_hbm, wd_hbm, y_hbm,
                xbuf, ybuf, wg_f, wu_f, wd_f, wg_s, wu_s, wd_s, in_sem, out_sem, w_sem, *, half, layer):
    n_active = blk_ref[BLK_N_ACTIVE, 0]

    def in_copy(g, slot):
        return pltpu.make_async_copy(xg_hbm.at[pl.ds(g * ROW_BLOCK, ROW_BLOCK)], xbuf.at[slot], in_sem.at[slot])

    def out_copy(g, slot):
        return pltpu.make_async_copy(ybuf.at[slot], y_hbm.at[pl.ds(g * ROW_BLOCK, ROW_BLOCK)], out_sem.at[slot])

    def weight_copies(e):
        return (pltpu.make_async_copy(wg_hbm.at[layer, e], wg_f, w_sem.at[0]),
                pltpu.make_async_copy(wu_hbm.at[layer, e], wu_f, w_sem.at[1]),
                pltpu.make_async_copy(wd_hbm.at[layer, e], wd_f, w_sem.at[2]))

    def sub_blocks(slot, n_sub, valid):
        hidden = []
        for s in range(n_sub):
            rs = slice(s * FFN_SUB, (s + 1) * FFN_SUB)
            keep = lax.broadcasted_iota(jnp.int32, (FFN_SUB, 1), 0) + s * FFN_SUB < valid
            lo, hi = _unpack_pair(xbuf[slot, rs, :])
            lo = jnp.where(keep, lo, 0.0).astype(_BF16)
            hi = jnp.where(keep, hi, 0.0).astype(_BF16)
            g = _dot(lo, wg_s[0:half, :]) + _dot(hi, wg_s[half:2 * half, :])
            u = _dot(lo, wu_s[0:half, :]) + _dot(hi, wu_s[half:2 * half, :])
            hidden.append((jax.nn.silu(g) * u).astype(_BF16))
        for s, h in enumerate(hidden):
            rs = slice(s * FFN_SUB, (s + 1) * FFN_SUB)
            ybuf[slot, rs, :] = _pack_pair(_dot(h, wd_s[:, 0:half]), _dot(h, wd_s[:, half:2 * half]))

    def block(g, slot):
        e = blk_ref[BLK_EXPERT, g]
        valid = blk_ref[BLK_VALID, g]

        @pl.when(g + FFN_SLOTS - 1 < n_active)
        def _():
            in_copy(g + FFN_SLOTS - 1, (slot + FFN_SLOTS - 1) % FFN_SLOTS).start()

        @pl.when((g == 0) | (e != blk_ref[BLK_EXPERT, jnp.maximum(g - 1, 0)]))
        def _():
            for c in weight_copies(e):
                c.wait()
            wg_s[...] = wg_f[...].astype(_BF16)
            wu_s[...] = wu_f[...].astype(_BF16)
            wd_s[...] = wd_f[...].astype(_BF16)
            nxt = blk_ref[BLK_REGION_END, g]

            @pl.when(nxt < n_active)
            def _():
                for c in weight_copies(blk_ref[BLK_EXPERT, nxt]):
                    c.start()

        in_copy(g, slot).wait()

        @pl.when(g >= FFN_SLOTS)
        def _():
            out_copy(g - FFN_SLOTS, slot).wait()

        @pl.when(valid > FFN_SUB)
        def _():
            sub_blocks(slot, ROW_BLOCK // FFN_SUB, valid)

        @pl.when(valid <= FFN_SUB)
        def _():
            sub_blocks(slot, 1, valid)

        out_copy(g, slot).start()

    for g in range(FFN_SLOTS - 1):
        @pl.when(g < n_active)
        def _():
            in_copy(g, g).start()
    for c in weight_copies(blk_ref[BLK_EXPERT, 0]):
        c.start()

    def ring(p, carry):
        for slot in range(FFN_SLOTS):
            g = FFN_SLOTS * p + slot

            @pl.when(g < n_active)
            def _():
                block(g, slot)

        return carry

    n_rings = lax.shift_right_logical(n_active + (FFN_SLOTS - 1), FFN_SLOTS.bit_length() - 1)
    lax.fori_loop(0, n_rings, ring, 0)

    for back in range(FFN_SLOTS):
        g = n_active - 1 - back

        @pl.when(g >= 0)
        def _():
            out_copy(g, lax.rem(g, FFN_SLOTS)).wait()


def _expert_ffn(blk, xg, w_gate, w_up, w_down, *, n_blk, layer):
    _, n_exp, d, hid = w_gate.shape
    half = d // 2
    kern = functools.partial(_ffn_kernel, half=half, layer=layer)
    any_spec = pl.BlockSpec(memory_space=pl.ANY)
    grid_spec = pltpu.PrefetchScalarGridSpec(
        num_scalar_prefetch=1,
        grid=(1,),
        in_specs=[any_spec, any_spec, any_spec, any_spec],
        out_specs=any_spec,
        scratch_shapes=[pltpu.VMEM((FFN_SLOTS, ROW_BLOCK, half), jnp.uint32),
                        pltpu.VMEM((FFN_SLOTS, ROW_BLOCK, half), jnp.uint32),
                        pltpu.VMEM((d, hid), _F32), pltpu.VMEM((d, hid), _F32), pltpu.VMEM((hid, d), _F32),
                        pltpu.VMEM((d, hid), _BF16), pltpu.VMEM((d, hid), _BF16), pltpu.VMEM((hid, d), _BF16),
                        pltpu.SemaphoreType.DMA((FFN_SLOTS,)), pltpu.SemaphoreType.DMA((FFN_SLOTS,)),
                        pltpu.SemaphoreType.DMA((3,))],
    )
    return pl.pallas_call(
        kern,
        grid_spec=grid_spec,
        out_shape=jax.ShapeDtypeStruct((n_blk * ROW_BLOCK, half), jnp.uint32),
        compiler_params=_params(),
        name="moe_expert_ffn",
    )(blk, xg, w_gate, w_up, w_down)


def _combine_kernel(x_ref, yg_ref, gate_ref, sg_ref, su_ref, sd_ref, g_ref, b_ref, o_ref, gpad_ref, *,
                    tm, d, alpha):
    x = x_ref[...]
    xb = x.astype(_BF16)
    half = d // 2
    gpad_ref[...] = jnp.zeros_like(gpad_ref)
    gpad_ref[0:TOP_K, :] = gate_ref[...]
    gate_t = gpad_ref[...].T
    lo_acc = jnp.zeros((tm, half), _F32)
    hi_acc = jnp.zeros((tm, half), _F32)
    for k in range(TOP_K):
        lo, hi = _unpack_pair(yg_ref[k])
        w = gate_t[:, k:k + 1]
        lo_acc = lo_acc + w * lo
        hi_acc = hi_acc + w * hi
    hs = (jax.nn.silu(_dot(xb, sg_ref[...])) * _dot(xb, su_ref[...])).astype(_BF16)
    shared = _dot(hs, sd_ref[...])
    f = jnp.concatenate([lo_acc, hi_acc], axis=1) + shared
    o_ref[...] = _layer_norm(alpha * x + f, g_ref[...], b_ref[...])


def _combine(x1, yg, gate, sh_gate, sh_up, sh_down, ln_g, ln_b, *, alpha, tm=512):
    n, d = x1.shape
    tm = min(tm, n)
    hid = sh_gate.shape[1]
    kern = functools.partial(_combine_kernel, tm=tm, d=d, alpha=alpha)
    return pl.pallas_call(
        kern,
        grid=(n // tm,),
        in_specs=[_rows(tm, d), pl.BlockSpec((TOP_K, tm, d // 2), lambda i: (0, i, 0)),
                  pl.BlockSpec((TOP_K, tm), lambda i: (0, i)), _full((d, hid)), _full((d, hid)),
                  _full((hid, d)), _full((1, d)), _full((1, d))],
        out_specs=_rows(tm, d),
        out_shape=jax.ShapeDtypeStruct((n, d), _F32),
        scratch_shapes=[pltpu.VMEM((LANES, tm), _F32)],
        compiler_params=_params(),
        name="moe_combine",
    )(x1, yg, gate, sh_gate.astype(_BF16), sh_up.astype(_BF16), sh_down.astype(_BF16),
      ln_g.reshape(1, d), ln_b.reshape(1, d))


SC_CORES = 2
SC_SUBCORES = 16
SC_WORKERS = SC_CORES * SC_SUBCORES
SC_WINDOW = 64


def _sc_mesh():
    return plsc.VectorSubcoreMesh(core_axis_name="c", subcore_axis_name="s",
                                  num_cores=SC_CORES, num_subcores=SC_SUBCORES)


def _sc_worker_id():
    return lax.axis_index("s") * SC_CORES + lax.axis_index("c")


def _dispatch_rows(xp, dest, n_rows):
    n, width = xp.shape
    n_slots = dest.shape[0]
    n_win = n // SC_WINDOW
    per_worker = n_win // SC_WORKERS
    assert per_worker * SC_WORKERS * SC_WINDOW == n and per_worker % 2 == 0
    dest_w = dest.reshape(n_slots, n_win, SC_WINDOW).transpose(1, 0, 2)

    def body(x_hbm, d_hbm, out_hbm, idx_v, rows_v, row_sem, idx_sem, out_sem):
        first = _sc_worker_id() * per_worker

        def load_rows(j, b):
            return pltpu.make_async_copy(x_hbm.at[pl.ds((first + j) * SC_WINDOW, SC_WINDOW)], rows_v.at[b],
                                         row_sem.at[b])

        def load_idx(j, b):
            return pltpu.make_async_copy(d_hbm.at[first + j], idx_v.at[b], idx_sem.at[b])

        def scatter(b, k):
            return pltpu.make_async_copy(rows_v.at[b], out_hbm.at[idx_v.at[b].at[k]], out_sem.at[b])

        for b in range(2):
            load_rows(b, b).start()
            load_idx(b, b).start()

        @pl.loop(0, per_worker, step=2)
        def _(j0):
            for b in range(2):
                j = j0 + b
                load_rows(j, b).wait()
                load_idx(j, b).wait()
                for k in range(n_slots):
                    scatter(b, k).start()
                for k in range(n_slots):
                    scatter(b, k).wait()

                @pl.when(j + 2 < per_worker)
                def _():
                    load_rows(j + 2, b).start()
                    load_idx(j + 2, b).start()

    return pl.kernel(
        body,
        out_type=jax.ShapeDtypeStruct((n_rows, width), xp.dtype),
        mesh=_sc_mesh(),
        scratch_types=[pltpu.VMEM((2, n_slots, SC_WINDOW), jnp.int32), pltpu.VMEM((2, SC_WINDOW, width), xp.dtype),
                       pltpu.SemaphoreType.DMA((2,)), pltpu.SemaphoreType.DMA((2,)), pltpu.SemaphoreType.DMA((2,))],
        name="moe_dispatch_rows",
    )(xp, dest_w)


def _gather_rows(yp, dest):
    n_slots, n = dest.shape
    width = yp.shape[1]
    m = n_slots * n
    per_worker = m // SC_WORKERS
    n_win = per_worker // SC_WINDOW
    assert n_win * SC_WINDOW * SC_WORKERS == m and n_win % 2 == 0

    def body(y_hbm, idx_hbm, out_hbm, idx_v, rows_v, in_sem, out_sem):
        base = _sc_worker_id() * per_worker
        pltpu.sync_copy(idx_hbm.at[pl.ds(base, per_worker)], idx_v)

        def gather(w, b):
            return pltpu.make_async_copy(y_hbm.at[idx_v.at[pl.ds(w * SC_WINDOW, SC_WINDOW)]], rows_v.at[b],
                                         in_sem.at[b])

        def put(w, b):
            return pltpu.make_async_copy(rows_v.at[b], out_hbm.at[pl.ds(base + w * SC_WINDOW, SC_WINDOW)],
                                         out_sem.at[b])

        for b in range(2):
            gather(b, b).start()

        @pl.loop(0, n_win, step=2)
        def _(w0):
            for b in range(2):
                w = w0 + b
                gather(w, b).wait()
                put(w, b).start()

                @pl.when(w >= 1)
                def _():
                    put(w - 1, 1 - b).wait()

                    @pl.when(w + 1 < n_win)
                    def _():
                        gather(w + 1, 1 - b).start()

        put(n_win - 1, 1).wait()

    out = pl.kernel(
        body,
        out_type=jax.ShapeDtypeStruct((m, width), yp.dtype),
        mesh=_sc_mesh(),
        scratch_types=[pltpu.VMEM((per_worker,), jnp.int32), pltpu.VMEM((2, SC_WINDOW, width), yp.dtype),
                       pltpu.SemaphoreType.DMA((2,)), pltpu.SemaphoreType.DMA((2,))],
        name="moe_gather_rows",
    )(yp, dest.reshape(m))
    return out.reshape(n_slots, n, width)


def _moe(x1, x1p, router_w, router_bias, w_gate, w_up, w_down, sh_gate, sh_up, sh_down, ln_g, ln_b, *, alpha,
         layer):
    n, d = x1.shape
    n_exp = router_w.shape[1]
    n_blk = (n * TOP_K + n_exp * (ROW_BLOCK - 1) + ROW_BLOCK - 1) // ROW_BLOCK
    eidx, gate, rank, cnt = _router(x1, router_w, router_bias)
    dest, blk = _tables(cnt, eidx, rank, n_blk=n_blk)
    xg = _dispatch_rows(x1p, dest, n_blk * ROW_BLOCK)
    y = _expert_ffn(blk, xg, w_gate, w_up, w_down, n_blk=n_blk, layer=layer)
    yg = _gather_rows(y, dest)
    return _combine(x1, yg, gate, sh_gate, sh_up, sh_down, ln_g, ln_b, alpha=alpha)


def kernel(x, a_w_in, a_conv_w, a_w_out, b_w_in, b_v_ln_g, b_v_ln_b, b_w_s, b_s_bias, b_w_out, c_w_in, c_w_grp,
           c_scale, c_w_out, d_w_in, d_conv_w, d_conv_b, d_ln_g, d_ln_b, d_w_out, ln1_g, ln1_b, ln2_g, ln2_b,
           router_w, router_bias, exp_w_gate, exp_w_up, exp_w_down, sh_w_gate, sh_w_up, sh_w_down):
    bsz, seq, d = x.shape
    depth = ln1_g.shape[0]
    n_mixers = 4
    alpha = (2 * depth) ** 0.25
    h = x.reshape(bsz * seq, d)
    for i in range(depth):
        m, j = i % n_mixers, i // n_mixers
        if m == 0:
            x1, x1p = _mixer_a(h, a_w_in[j], a_conv_w[j], a_w_out[j], ln1_g[i], ln1_b[i], seq=seq, alpha=alpha)
        elif m == 1:
            x1, x1p = _mixer_b(h, b_w_in[j], b_v_ln_g[j], b_v_ln_b[j], b_w_s[j], b_s_bias[j], b_w_out[j],
                               ln1_g[i], ln1_b[i], alpha=alpha)
        elif m == 2:
            x1, x1p = _mixer_c(h, c_w_in[j], c_w_grp[j], c_scale[j], c_w_out[j], ln1_g[i], ln1_b[i],
                               seq=seq, alpha=alpha)
        else:
            x1, x1p = _mixer_d(h, d_w_in[j], d_conv_w[j], d_conv_b[j], d_ln_g[j], d_ln_b[j], d_w_out[j],
                               ln1_g[i], ln1_b[i], seq=seq, alpha=alpha)
        h = _moe(x1, x1p, router_w[i], router_bias[i], exp_w_gate, exp_w_up, exp_w_down,
                 sh_w_gate[i], sh_w_up[i], sh_w_down[i], ln2_g[i], ln2_b[i], alpha=alpha, layer=i)
    return h.reshape(bsz, seq, d)
```

```python
import functools

import jax
import jax.numpy as jnp
from jax import lax
from jax.experimental import pallas as pl
from jax.experimental.pallas import tpu as pltpu
from jax.experimental.pallas import tpu_sc as plsc

LN_EPS = 1e-5
CHUNK = 128
GMLP_HEADS = 8
POOL_WINDOWS = (2, 4, 8, 16)
N_EXPERT_GROUPS = 8
TOPK_GROUPS = 4
TOP_K = 8
ROUTED_SCALE = 2.5
ROW_BLOCK = 512
FFN_SUB = 256
COMBINE_CHUNKS = 2
FFN_SLOTS = 4
BLK_EXPERT, BLK_VALID, BLK_REGION_END, BLK_N_ACTIVE, BLK_ROWS_USED = 0, 1, 2, 3, 4
LANES = 128
SUBLANES = 8
VMEM_LIMIT = 56 * 1024 * 1024

_BF16 = jnp.bfloat16
_F32 = jnp.float32


def _dot(a, b):
    return jnp.dot(a, b, preferred_element_type=_F32)


def _layer_norm(z, g, b):
    mu = jnp.mean(z, axis=-1, keepdims=True)
    zc = z - mu
    var = jnp.mean(zc * zc, axis=-1, keepdims=True)
    return zc * lax.rsqrt(var + LN_EPS) * g + b


def _pack_pair(lo, hi):
    lo_bits = lax.bitcast_convert_type(lo.astype(_BF16).astype(_F32), jnp.uint32)
    hi_bits = lax.bitcast_convert_type(hi.astype(_BF16).astype(_F32), jnp.uint32)
    return lax.shift_right_logical(lo_bits, jnp.uint32(16)) | (hi_bits & jnp.uint32(0xFFFF0000))


def _unpack_pair(p):
    lo = lax.bitcast_convert_type(lax.shift_left(p, jnp.uint32(16)), _F32)
    hi = lax.bitcast_convert_type(p & jnp.uint32(0xFFFF0000), _F32)
    return lo, hi


def _store_residual_ln(x_ref, y, rows, g_ref, b_ref, alpha, o_ref, op_ref):
    x1 = _layer_norm(alpha * x_ref[rows, :] + y, g_ref[...], b_ref[...])
    o_ref[rows, :] = x1
    half = x1.shape[-1] // 2
    op_ref[rows, :] = _pack_pair(x1[:, :half], x1[:, half:])


def _params(n_axes=1):
    return pltpu.CompilerParams(dimension_semantics=("arbitrary",) * n_axes, vmem_limit_bytes=VMEM_LIMIT)


def _full(shape):
    nd = len(shape)
    return pl.BlockSpec(shape, lambda i, _nd=nd: (0,) * _nd, pipeline_mode=pl.Buffered(1))


def _rows(tm, width):
    return pl.BlockSpec((tm, width), lambda i: (i, 0))


def _mixer_out_shapes(n, d):
    return (jax.ShapeDtypeStruct((n, d), _F32), jax.ShapeDtypeStruct((n, d // 2), jnp.uint32))


def _mixer_a_kernel(x_ref, win_ref, cw_ref, wout_ref, g_ref, b_ref, o_ref, op_ref, zext_ref, *,
                    tm, d, tiles_per_seq, alpha, halo, row_group):
    i = pl.program_id(0)

    @pl.when(i % tiles_per_seq == 0)
    def _():
        zext_ref[0:halo, :] = jnp.zeros((halo, d), _F32)

    @pl.when(i % tiles_per_seq != 0)
    def _():
        zext_ref[0:halo, :] = zext_ref[tm:tm + halo, :]

    groups = range(0, tm, row_group)
    gates = []
    for g0 in groups:
        xb = x_ref[g0:g0 + row_group, :].astype(_BF16)
        gates.append(_dot(xb, win_ref[:, 0:d]))
        zext_ref[halo + g0:halo + g0 + row_group, :] = (
            _dot(xb, win_ref[:, d:2 * d]) * _dot(xb, win_ref[:, 2 * d:3 * d]))
    width = cw_ref.shape[0]
    for g0, b_gate in zip(groups, gates):
        conv = jnp.zeros((row_group, d), _F32)
        for k in range(width):
            off = halo - (width - 1) + k + g0
            conv = conv + cw_ref[k:k + 1, :] * zext_ref[off:off + row_group, :]
        y = _dot((b_gate * conv).astype(_BF16), wout_ref[...])
        rows = slice(g0, g0 + row_group)
        _store_residual_ln(x_ref, y, rows, g_ref, b_ref, alpha, o_ref, op_ref)


def _mixer_a(x2, w_in, conv_w, w_out, ln_g, ln_b, *, seq, alpha, tm=512, row_group=256):
    n, d = x2.shape
    tm = min(tm, seq)
    row_group = min(row_group, tm)
    halo = 8
    assert conv_w.shape[0] - 1 <= halo
    kern = functools.partial(_mixer_a_kernel, tm=tm, d=d, tiles_per_seq=seq // tm, alpha=alpha, halo=halo,
                             row_group=row_group)
    return pl.pallas_call(
        kern,
        grid=(n // tm,),
        in_specs=[_rows(tm, d), _full(w_in.shape), _full(conv_w.shape), _full(w_out.shape),
                  _full((1, d)), _full((1, d))],
        out_specs=(_rows(tm, d), _rows(tm, d // 2)),
        out_shape=_mixer_out_shapes(n, d),
        scratch_shapes=[pltpu.VMEM((tm + halo, d), _F32)],
        compiler_params=_params(),
        name="mixer_a",
    )(x2, w_in.astype(_BF16), conv_w, w_out.astype(_BF16), ln_g.reshape(1, d), ln_b.reshape(1, d))


def _mixer_b_kernel(x_ref, win_ref, vg_ref, vb_ref, ws_ref, sbt_ref, wout_ref, g_ref, b_ref, o_ref, op_ref,
                    gated_ref, *, tm, half, heads, alpha, row_group):
    groups = [slice(r, r + row_group) for r in range(0, tm, row_group)]
    us, vns = [], []
    for rows in groups:
        xb = x_ref[rows, :].astype(_BF16)
        us.append(jax.nn.gelu(_dot(xb, win_ref[:, 0:half])))
        v = jax.nn.gelu(_dot(xb, win_ref[:, half:2 * half]))
        vns.append(_layer_norm(v, vg_ref[...], vb_ref[...]).astype(_BF16))
    hd = half // heads
    row = lax.broadcasted_iota(jnp.int32, (CHUNK, CHUNK), 0)
    col = lax.broadcasted_iota(jnp.int32, (CHUNK, CHUNK), 1)
    causal = row >= col
    for h in range(heads):
        w_c = jnp.where(causal, ws_ref[h], 0.0).astype(_BF16)
        bias = sbt_ref[:, h:h + 1]
        cs = slice(h * hd, (h + 1) * hd)
        for rows, u, vn in zip(groups, us, vns):
            for c in range(row_group // CHUNK):
                rs = slice(c * CHUNK, (c + 1) * CHUNK)
                s = _dot(w_c, vn[rs, cs]) + bias
                gated_ref[rows.start + c * CHUNK:rows.start + (c + 1) * CHUNK, cs] = (u[rs, cs] * s).astype(_BF16)
    for rows in groups:
        y = _dot(gated_ref[rows, :], wout_ref[...])
        _store_residual_ln(x_ref, y, rows, g_ref, b_ref, alpha, o_ref, op_ref)


def _mixer_b(x2, w_in, v_ln_g, v_ln_b, w_s, s_bias, w_out, ln_g, ln_b, *, alpha, tm=512, row_group=256):
    n, d = x2.shape
    half = w_in.shape[1] // 2
    heads = w_s.shape[0]
    kern = functools.partial(_mixer_b_kernel, tm=tm, half=half, heads=heads, alpha=alpha, row_group=row_group)
    return pl.pallas_call(
        kern,
        grid=(n // tm,),
        in_specs=[_rows(tm, d), _full(w_in.shape), _full((1, half)), _full((1, half)), _full(w_s.shape),
                  _full((CHUNK, heads)), _full(w_out.shape), _full((1, d)), _full((1, d))],
        out_specs=(_rows(tm, d), _rows(tm, d // 2)),
        out_shape=_mixer_out_shapes(n, d),
        scratch_shapes=[pltpu.VMEM((tm, half), _BF16)],
        compiler_params=_params(),
        name="mixer_b",
    )(x2, w_in.astype(_BF16), v_ln_g.reshape(1, half), v_ln_b.reshape(1, half), w_s, s_bias.T,
      w_out.astype(_BF16), ln_g.reshape(1, d), ln_b.reshape(1, d))


def _mixer_c_kernel(x_ref, win_ref, wgrp_ref, scale_ref, wout_ref, g_ref, b_ref, o_ref, op_ref, hext_ref, *,
                    tm, d, tiles_per_seq, alpha, halo, row_group):
    i = pl.program_id(0)

    @pl.when(i % tiles_per_seq == 0)
    def _():
        hext_ref[0:halo, :] = jnp.zeros((halo, d), _F32)

    @pl.when(i % tiles_per_seq != 0)
    def _():
        hext_ref[0:halo, :] = hext_ref[tm:tm + halo, :]

    groups = range(0, tm, row_group)
    for g0 in groups:
        hext_ref[halo + g0:halo + g0 + row_group, :] = _dot(x_ref[g0:g0 + row_group, :].astype(_BF16), win_ref[...])
    gw = d // len(POOL_WINDOWS)
    for g0 in groups:
        pos = (i % tiles_per_seq) * tm + g0 + lax.broadcasted_iota(jnp.int32, (row_group, 1), 0) + 1
        y = jnp.zeros((row_group, d), _F32)
        for g, win in enumerate(POOL_WINDOWS):
            cs = slice(g * gw, (g + 1) * gw)
            h = hext_ref[halo + g0:halo + g0 + row_group, cs]
            acc = h
            for j in range(1, win):
                acc = acc + hext_ref[halo + g0 - j:halo + g0 - j + row_group, cs]
            mean = acc / jnp.minimum(pos, win).astype(_F32)
            p = mean - h
            yg = _dot(p.astype(_BF16), wgrp_ref[g]) * scale_ref[:, cs]
            y = y + _dot(yg.astype(_BF16), wout_ref[cs, :])
        rows = slice(g0, g0 + row_group)
        _store_residual_ln(x_ref, y, rows, g_ref, b_ref, alpha, o_ref, op_ref)


def _mixer_c(x2, w_in, w_grp, scale, w_out, ln_g, ln_b, *, seq, alpha, tm=512, row_group=256):
    n, d = x2.shape
    tm = min(tm, seq)
    row_group = min(row_group, tm)
    halo = 16
    assert max(POOL_WINDOWS) <= halo
    kern = functools.partial(_mixer_c_kernel, tm=tm, d=d, tiles_per_seq=seq // tm, alpha=alpha, halo=halo,
                             row_group=row_group)
    return pl.pallas_call(
        kern,
        grid=(n // tm,),
        in_specs=[_rows(tm, d), _full(w_in.shape), _full(w_grp.shape), _full((1, d)), _full(w_out.shape),
                  _full((1, d)), _full((1, d))],
        out_specs=(_rows(tm, d), _rows(tm, d // 2)),
        out_shape=_mixer_out_shapes(n, d),
        scratch_shapes=[pltpu.VMEM((tm + halo, d), _F32)],
        compiler_params=_params(),
        name="mixer_c",
    )(x2, w_in.astype(_BF16), w_grp.astype(_BF16), scale.reshape(1, d), w_out.astype(_BF16),
      ln_g.reshape(1, d), ln_b.reshape(1, d))


def _mixer_d_kernel(x_ref, win_ref, cw_ref, cb_ref, cg_ref, cbeta_ref, wout_ref, g_ref, b_ref, o_ref, op_ref,
                    hext_ref, shift_ref, conv_ref, *, tm, d, tiles_per_seq, alpha, halo, row_chunk, lane_chunk,
                    row_group):
    i = pl.program_id(0)

    @pl.when(i % tiles_per_seq == 0)
    def _():
        hext_ref[0:halo, :] = jnp.zeros((halo, d), _F32)

    @pl.when(i % tiles_per_seq != 0)
    def _():
        hext_ref[0:halo, :] = hext_ref[tm:tm + halo, :]

    groups = range(0, tm, row_group)
    for g0 in groups:
        xb = x_ref[g0:g0 + row_group, :].astype(_BF16)
        hext_ref[halo + g0:halo + g0 + row_group, :] = (
            _dot(xb, win_ref[:, 0:d]) * jax.nn.sigmoid(_dot(xb, win_ref[:, d:2 * d])))
    width = cw_ref.shape[0]
    shift_rows = row_group + halo - SUBLANES
    for g0 in groups:
        for r in range(1, SUBLANES):
            shift_ref[r - 1, g0:g0 + shift_rows, :] = hext_ref[g0 + r:g0 + r + shift_rows, :]
        for lc in range(d // lane_chunk):
            cs = slice(lc * lane_chunk, (lc + 1) * lane_chunk)
            bias = cb_ref[:, cs]
            for r0 in range(g0, g0 + row_group, row_chunk):
                acc = jnp.zeros((row_chunk, lane_chunk), _F32) + bias
                for k in range(width):
                    q, r = divmod(halo - (width - 1) + k, SUBLANES)
                    off = q * SUBLANES + r0
                    if r == 0:
                        src = hext_ref[off:off + row_chunk, cs]
                    else:
                        src = shift_ref[r - 1, off:off + row_chunk, cs]
                    acc = acc + cw_ref[k:k + 1, cs] * src
                conv_ref[r0:r0 + row_chunk, cs] = acc
    for g0 in groups:
        rows = slice(g0, g0 + row_group)
        c = _layer_norm(conv_ref[rows, :], cg_ref[...], cbeta_ref[...])
        y = _dot(jax.nn.silu(c).astype(_BF16), wout_ref[...])
        _store_residual_ln(x_ref, y, rows, g_ref, b_ref, alpha, o_ref, op_ref)


def _mixer_d(x2, w_in, conv_w, conv_b, c_ln_g, c_ln_b, w_out, ln_g, ln_b, *, seq, alpha, tm=512, row_group=256):
    n, d = x2.shape
    tm = min(tm, seq)
    row_group = min(row_group, tm)
    halo = 32
    assert conv_w.shape[0] - 1 <= halo
    kern = functools.partial(_mixer_d_kernel, tm=tm, d=d, tiles_per_seq=seq // tm, alpha=alpha, halo=halo,
                             row_chunk=64, lane_chunk=LANES, row_group=row_group)
    return pl.pallas_call(
        kern,
        grid=(n // tm,),
        in_specs=[_rows(tm, d), _full(w_in.shape), _full(conv_w.shape), _full((1, d)), _full((1, d)),
                  _full((1, d)), _full(w_out.shape), _full((1, d)), _full((1, d))],
        out_specs=(_rows(tm, d), _rows(tm, d // 2)),
        out_shape=_mixer_out_shapes(n, d),
        scratch_shapes=[pltpu.VMEM((tm + halo, d), _F32),
                        pltpu.VMEM((SUBLANES - 1, tm + halo - SUBLANES, d), _F32),
                        pltpu.VMEM((tm, d), _F32)],
        compiler_params=_params(),
        name="mixer_d",
    )(x2, w_in.astype(_BF16), conv_w, conv_b.reshape(1, d), c_ln_g.reshape(1, d), c_ln_b.reshape(1, d),
      w_out.astype(_BF16), ln_g.reshape(1, d), ln_b.reshape(1, d))


def _router_kernel(x_ref, w2_ref, bias_ref, eidx_ref, gate_ref, rank_ref, cnt_ref,
                   carry_ref, gs_ref, masked_ref, graw_ref, *, tm, n_exp):
    i = pl.program_id(0)

    @pl.when(i == 0)
    def _():
        carry_ref[...] = jnp.zeros_like(carry_ref)

    x = x_ref[...]
    x_hi = x.astype(_BF16)
    x_lo = (x - x_hi.astype(_F32)).astype(_BF16)
    parts = (_dot(x_hi, w2_ref[...]) + _dot(x_lo, w2_ref[...])).T
    logits = parts[0:n_exp, :] + parts[n_exp:2 * n_exp, :]
    scores = jax.nn.sigmoid(logits)
    biased = scores + bias_ref[...]
    per = n_exp // N_EXPERT_GROUPS
    neg = -jnp.inf
    sub = lax.broadcasted_iota(jnp.int32, (per, tm), 0)
    for g in range(N_EXPERT_GROUPS):
        bg = biased[g * per:(g + 1) * per, :]
        m1 = jnp.max(bg, axis=0, keepdims=True)
        first = jnp.min(jnp.where(bg == m1, sub, per), axis=0, keepdims=True)
        m2 = jnp.max(jnp.where(sub == first, neg, bg), axis=0, keepdims=True)
        gs_ref[g:g + 1, :] = m1 + m2

    giota = lax.broadcasted_iota(jnp.int32, (N_EXPERT_GROUPS, tm), 0)
    cur = gs_ref[...]
    gsel = jnp.zeros((N_EXPERT_GROUPS, tm), jnp.bool_)
    for _ in range(TOPK_GROUPS):
        m = jnp.max(cur, axis=0, keepdims=True)
        gi = jnp.min(jnp.where(cur == m, giota, N_EXPERT_GROUPS), axis=0, keepdims=True)
        pick = giota == gi
        gsel = gsel | pick
        cur = jnp.where(pick, neg, cur)
    gs_ref[...] = gsel.astype(_F32)
    for g in range(N_EXPERT_GROUPS):
        keep = gs_ref[g:g + 1, :] > 0.5
        masked_ref[g * per:(g + 1) * per, :] = jnp.where(keep, biased[g * per:(g + 1) * per, :], neg)

    eiota = lax.broadcasted_iota(jnp.int32, (n_exp, tm), 0)
    cur = masked_ref[...]
    sel = jnp.zeros((n_exp, tm), jnp.bool_)
    for k in range(TOP_K):
        m = jnp.max(cur, axis=0, keepdims=True)
        ei = jnp.min(jnp.where(cur == m, eiota, n_exp), axis=0, keepdims=True)
        pick = eiota == ei
        eidx_ref[k:k + 1, :] = ei
        graw_ref[k:k + 1, :] = jnp.sum(jnp.where(pick, scores, 0.0), axis=0, keepdims=True)
        sel = sel | pick
        cur = jnp.where(pick, neg, cur)
    graw = graw_ref[...]
    gate_ref[...] = graw / jnp.sum(graw, axis=0, keepdims=True) * ROUTED_SCALE

    sel_b = sel.astype(_F32).astype(_BF16)
    before = (lax.broadcasted_iota(jnp.int32, (tm, tm), 0)
              < lax.broadcasted_iota(jnp.int32, (tm, tm), 1)).astype(_F32).astype(_BF16)
    rank_all = _dot(sel_b, before) + carry_ref[:, 0:1]
    for k in range(TOP_K):
        pick = eiota == eidx_ref[k:k + 1, :]
        rank_ref[k:k + 1, :] = jnp.sum(jnp.where(pick, rank_all, 0.0), axis=0, keepdims=True).astype(jnp.int32)
    carry_ref[...] = carry_ref[...] + jnp.sum(sel.astype(_F32), axis=1, keepdims=True)
    cnt_ref[...] = carry_ref[...]


def _router(x1, router_w, router_bias, *, tm=512):
    n, d = x1.shape
    n_exp = router_w.shape[1]
    tm = min(tm, n)
    kern = functools.partial(_router_kernel, tm=tm, n_exp=n_exp)
    tok = pl.BlockSpec((TOP_K, tm), lambda i: (0, i))
    w_hi = router_w.astype(_BF16)
    w2 = jnp.concatenate([w_hi, (router_w - w_hi.astype(_F32)).astype(_BF16)], axis=1)
    return pl.pallas_call(
        kern,
        grid=(n // tm,),
        in_specs=[_rows(tm, d), _full((d, 2 * n_exp)), _full((n_exp, 1))],
        out_specs=(tok, tok, tok, pl.BlockSpec((n_exp, LANES), lambda i: (0, 0))),
        out_shape=(jax.ShapeDtypeStruct((TOP_K, n), jnp.int32), jax.ShapeDtypeStruct((TOP_K, n), _F32),
                   jax.ShapeDtypeStruct((TOP_K, n), jnp.int32), jax.ShapeDtypeStruct((n_exp, LANES), _F32)),
        scratch_shapes=[pltpu.VMEM((n_exp, LANES), _F32), pltpu.VMEM((N_EXPERT_GROUPS, tm), _F32),
                        pltpu.VMEM((n_exp, tm), _F32), pltpu.VMEM((TOP_K, tm), _F32)],
        compiler_params=_params(),
        name="moe_router",
    )(x1, w2, router_bias.reshape(n_exp, 1))


def _tables_kernel(cnt_ref, eidx_ref, rank_ref, dest_ref, blk_ref, *, n_exp, n_blk_pad):
    counts = cnt_ref[...].astype(jnp.int32)
    shift = ROW_BLOCK.bit_length() - 1
    padded = lax.shift_left(lax.shift_right_logical(counts + (ROW_BLOCK - 1), shift), shift)
    lower = (lax.broadcasted_iota(jnp.int32, (n_exp, n_exp), 1)
             < lax.broadcasted_iota(jnp.int32, (n_exp, n_exp), 0)).astype(_F32)
    pad_start = jnp.dot(lower, padded.astype(_F32), precision=lax.Precision.HIGHEST,
                        preferred_element_type=_F32).astype(jnp.int32)
    pad_end = pad_start + padded
    eidx = eidx_ref[...]
    dest = rank_ref[...]
    for e in range(n_exp):
        dest = dest + jnp.where(eidx == e, pad_start[e:e + 1, 0:1], 0)
    dest_ref[...] = dest

    @pl.when(pl.program_id(0) == 0)
    def _():
        bstart = lax.broadcasted_iota(jnp.int32, (1, n_blk_pad), 1) * ROW_BLOCK
        blk_e = jnp.zeros((1, n_blk_pad), jnp.int32)
        for e in range(n_exp):
            blk_e = blk_e + (pad_end[e:e + 1, 0:1] <= bstart).astype(jnp.int32)
        blk_e = jnp.minimum(blk_e, n_exp - 1)
        used_end = pad_start + counts
        valid = jnp.zeros((1, n_blk_pad), jnp.int32)
        for e in range(n_exp):
            valid = valid + jnp.where(blk_e == e, used_end[e:e + 1, 0:1], 0)
        valid = jnp.clip(valid - bstart, 0, ROW_BLOCK)
        end_blk = lax.shift_right_logical(pad_end, shift)
        region_end = jnp.zeros((1, n_blk_pad), jnp.int32)
        for e in range(n_exp):
            region_end = region_end + jnp.where(blk_e == e, end_blk[e:e + 1, 0:1], 0)
        blk_ref[BLK_EXPERT:BLK_EXPERT + 1, :] = blk_e
        blk_ref[BLK_VALID:BLK_VALID + 1, :] = valid
        blk_ref[BLK_REGION_END:BLK_REGION_END + 1, :] = region_end
        blk_ref[BLK_N_ACTIVE:BLK_N_ACTIVE + 1, :] = jnp.broadcast_to(end_blk[n_exp - 1:n_exp, 0:1], (1, n_blk_pad))
        blk_ref[BLK_ROWS_USED:8, :] = jnp.zeros((8 - BLK_ROWS_USED, n_blk_pad), jnp.int32)


def _tables(cnt, eidx, rank, *, n_blk, tm=2048):
    n = eidx.shape[1]
    n_exp = cnt.shape[0]
    tm = min(tm, n)
    n_blk_pad = pl.cdiv(n_blk, LANES) * LANES
    kern = functools.partial(_tables_kernel, n_exp=n_exp, n_blk_pad=n_blk_pad)
    tok = pl.BlockSpec((TOP_K, tm), lambda i: (0, i))
    return pl.pallas_call(
        kern,
        grid=(n // tm,),
        in_specs=[_full((n_exp, LANES)), tok, tok],
        out_specs=(tok, pl.BlockSpec((8, n_blk_pad), lambda i: (0, 0))),
        out_shape=(jax.ShapeDtypeStruct((TOP_K, n), jnp.int32), jax.ShapeDtypeStruct((8, n_blk_pad), jnp.int32)),
        compiler_params=_params(),
        name="moe_tables",
    )(cnt, eidx, rank)


def _ffn_kernel(blk_ref, xg_hbm, wg_hbm, wu_hbm, wd_hbm, y_hbm,
                xbuf, ybuf, wg_f, wu_f, wd_f, wg_s, wu_s, wd_s, in_sem, out_sem, w_sem, *, half, layer):
    n_active = blk_ref[BLK_N_ACTIVE, 0]

    def in_copy(g, slot):
        return pltpu.make_async_copy(xg_hbm.at[pl.ds(g * ROW_BLOCK, ROW_BLOCK)], xbuf.at[slot], in_sem.at[slot])

    def out_copy(g, slot):
        return pltpu.make_async_copy(ybuf.at[slot], y_hbm.at[pl.ds(g * ROW_BLOCK, ROW_BLOCK)], out_sem.at[slot])

    def weight_copies(e):
        return (pltpu.make_async_copy(wg_hbm.at[layer, e], wg_f, w_sem.at[0]),
                pltpu.make_async_copy(wu_hbm.at[layer, e], wu_f, w_sem.at[1]),
                pltpu.make_async_copy(wd_hbm.at[layer, e], wd_f, w_sem.at[2]))

    def sub_blocks(slot, n_sub, valid):
        hidden = []
        for s in range(n_sub):
            rs = slice(s * FFN_SUB, (s + 1) * FFN_SUB)
            keep = lax.broadcasted_iota(jnp.int32, (FFN_SUB, 1), 0) + s * FFN_SUB < valid
            lo, hi = _unpack_pair(xbuf[slot, rs, :])
            lo = jnp.where(keep, lo, 0.0).astype(_BF16)
            hi = jnp.where(keep, hi, 0.0).astype(_BF16)
            g = _dot(lo, wg_s[0:half, :]) + _dot(hi, wg_s[half:2 * half, :])
            u = _dot(lo, wu_s[0:half, :]) + _dot(hi, wu_s[half:2 * half, :])
            hidden.append((jax.nn.silu(g) * u).astype(_BF16))
        for s, h in enumerate(hidden):
            rs = slice(s * FFN_SUB, (s + 1) * FFN_SUB)
            ybuf[slot, rs, :] = _pack_pair(_dot(h, wd_s[:, 0:half]), _dot(h, wd_s[:, half:2 * half]))

    def block(g, slot):
        e = blk_ref[BLK_EXPERT, g]
        valid = blk_ref[BLK_VALID, g]

        @pl.when(g + FFN_SLOTS - 1 < n_active)
        def _():
            in_copy(g + FFN_SLOTS - 1, (slot + FFN_SLOTS - 1) % FFN_SLOTS).start()

        @pl.when((g == 0) | (e != blk_ref[BLK_EXPERT, jnp.maximum(g - 1, 0)]))
        def _():
            for c in weight_copies(e):
                c.wait()
            wg_s[...] = wg_f[...].astype(_BF16)
            wu_s[...] = wu_f[...].astype(_BF16)
            wd_s[...] = wd_f[...].astype(_BF16)
            nxt = blk_ref[BLK_REGION_END, g]

            @pl.when(nxt < n_active)
            def _():
                for c in weight_copies(blk_ref[BLK_EXPERT, nxt]):
                    c.start()

        in_copy(g, slot).wait()

        @pl.when(g >= FFN_SLOTS)
        def _():
            out_copy(g - FFN_SLOTS, slot).wait()

        @pl.when(valid > FFN_SUB)
        def _():
            sub_blocks(slot, ROW_BLOCK // FFN_SUB, valid)

        @pl.when(valid <= FFN_SUB)
        def _():
            sub_blocks(slot, 1, valid)

        out_copy(g, slot).start()

    for g in range(FFN_SLOTS - 1):
        @pl.when(g < n_active)
        def _():
            in_copy(g, g).start()
    for c in weight_copies(blk_ref[BLK_EXPERT, 0]):
        c.start()

    def ring(p, carry):
        for slot in range(FFN_SLOTS):
            g = FFN_SLOTS * p + slot

            @pl.when(g < n_active)
            def _():
                block(g, slot)

        return carry

    n_rings = lax.shift_right_logical(n_active + (FFN_SLOTS - 1), FFN_SLOTS.bit_length() - 1)
    lax.fori_loop(0, n_rings, ring, 0)

    for back in range(FFN_SLOTS):
        g = n_active - 1 - back

        @pl.when(g >= 0)
        def _():
            out_copy(g, lax.rem(g, FFN_SLOTS)).wait()


def _expert_ffn(blk, xg, w_gate, w_up, w_down, *, n_blk, layer):
    _, n_exp, d, hid = w_gate.shape
    half = d // 2
    kern = functools.partial(_ffn_kernel, half=half, layer=layer)
    any_spec = pl.BlockSpec(memory_space=pl.ANY)
    grid_spec = pltpu.PrefetchScalarGridSpec(
        num_scalar_prefetch=1,
        grid=(1,),
        in_specs=[any_spec, any_spec, any_spec, any_spec],
        out_specs=any_spec,
        scratch_shapes=[pltpu.VMEM((FFN_SLOTS, ROW_BLOCK, half), jnp.uint32),
                        pltpu.VMEM((FFN_SLOTS, ROW_BLOCK, half), jnp.uint32),
                        pltpu.VMEM((d, hid), _F32), pltpu.VMEM((d, hid), _F32), pltpu.VMEM((hid, d), _F32),
                        pltpu.VMEM((d, hid), _BF16), pltpu.VMEM((d, hid), _BF16), pltpu.VMEM((hid, d), _BF16),
                        pltpu.SemaphoreType.DMA((FFN_SLOTS,)), pltpu.SemaphoreType.DMA((FFN_SLOTS,)),
                        pltpu.SemaphoreType.DMA((3,))],
    )
    return pl.pallas_call(
        kern,
        grid_spec=grid_spec,
        out_shape=jax.ShapeDtypeStruct((n_blk * ROW_BLOCK, half), jnp.uint32),
        compiler_params=_params(),
        name="moe_expert_ffn",
    )(blk, xg, w_gate, w_up, w_down)


def _combine_kernel(x_ref, yg_ref, gate_ref, sg_ref, su_ref, sd_ref, g_ref, b_ref, o_ref, gpad_ref, *,
                    tm, d, alpha):
    x = x_ref[...]
    xb = x.astype(_BF16)
    half = d // 2
    gpad_ref[...] = jnp.zeros_like(gpad_ref)
    gpad_ref[0:TOP_K, :] = gate_ref[...]
    gate_t = gpad_ref[...].T
    lo_acc = jnp.zeros((tm, half), _F32)
    hi_acc = jnp.zeros((tm, half), _F32)
    for k in range(TOP_K):
        lo, hi = _unpack_pair(yg_ref[k])
        w = gate_t[:, k:k + 1]
        lo_acc = lo_acc + w * lo
        hi_acc = hi_acc + w * hi
    hs = (jax.nn.silu(_dot(xb, sg_ref[...])) * _dot(xb, su_ref[...])).astype(_BF16)
    shared = _dot(hs, sd_ref[...])
    f = jnp.concatenate([lo_acc, hi_acc], axis=1) + shared
    o_ref[...] = _layer_norm(alpha * x + f, g_ref[...], b_ref[...])


def _combine_into_kernel(x_ref, yg_ref, gate_ref, sg_ref, su_ref, sd_ref, g_ref, b_ref, prev_ref, o_ref, gpad_ref,
                         **kw):
    del prev_ref
    _combine_kernel(x_ref, yg_ref, gate_ref, sg_ref, su_ref, sd_ref, g_ref, b_ref, o_ref, gpad_ref, **kw)


def _combine(x1, yg, gate, sh_gate, sh_up, sh_down, ln_g, ln_b, *, alpha, tm=512, first_row=0, into=None):
    n, d = x1.shape
    n_part = yg.shape[1]
    tm = min(tm, n_part)
    off = first_row // tm
    assert off * tm == first_row and n_part % tm == 0
    hid = sh_gate.shape[1]
    in_specs = [pl.BlockSpec((tm, d), lambda i: (i + off, 0)),
                pl.BlockSpec((TOP_K, tm, d // 2), lambda i: (0, i, 0)),
                pl.BlockSpec((TOP_K, tm), lambda i: (0, i + off)), _full((d, hid)), _full((d, hid)),
                _full((hid, d)), _full((1, d)), _full((1, d))]
    args = [x1, yg, gate, sh_gate.astype(_BF16), sh_up.astype(_BF16), sh_down.astype(_BF16),
            ln_g.reshape(1, d), ln_b.reshape(1, d)]
    body, aliases = _combine_kernel, {}
    if into is not None:
        in_specs.append(pl.BlockSpec(memory_space=pl.ANY))
        args.append(into)
        body, aliases = _combine_into_kernel, {len(args) - 1: 0}
    return pl.pallas_call(
        functools.partial(body, tm=tm, d=d, alpha=alpha),
        grid=(n_part // tm,),
        in_specs=in_specs,
        out_specs=pl.BlockSpec((tm, d), lambda i: (i + off, 0)),
        out_shape=jax.ShapeDtypeStruct((n, d), _F32),
        scratch_shapes=[pltpu.VMEM((LANES, tm), _F32)],
        input_output_aliases=aliases,
        compiler_params=_params(),
        name="moe_combine",
    )(*args)


SC_CORES = 2
SC_SUBCORES = 16
SC_WORKERS = SC_CORES * SC_SUBCORES
SC_WINDOW = 64


def _sc_mesh():
    return plsc.VectorSubcoreMesh(core_axis_name="c", subcore_axis_name="s",
                                  num_cores=SC_CORES, num_subcores=SC_SUBCORES)


def _sc_worker_id():
    return lax.axis_index("s") * SC_CORES + lax.axis_index("c")


def _dispatch_rows(xp, dest, n_rows):
    n, width = xp.shape
    n_slots = dest.shape[0]
    n_win = n // SC_WINDOW
    per_worker = n_win // SC_WORKERS
    assert per_worker * SC_WORKERS * SC_WINDOW == n and per_worker % 2 == 0
    dest_w = dest.reshape(n_slots, n_win, SC_WINDOW).transpose(1, 0, 2)

    def body(x_hbm, d_hbm, out_hbm, idx_v, rows_v, row_sem, idx_sem, out_sem):
        first = _sc_worker_id() * per_worker

        def load_rows(j, b):
            return pltpu.make_async_copy(x_hbm.at[pl.ds((first + j) * SC_WINDOW, SC_WINDOW)], rows_v.at[b],
                                         row_sem.at[b])

        def load_idx(j, b):
            return pltpu.make_async_copy(d_hbm.at[first + j], idx_v.at[b], idx_sem.at[b])

        def scatter(b, k):
            return pltpu.make_async_copy(rows_v.at[b], out_hbm.at[idx_v.at[b].at[k]], out_sem.at[b])

        for b in range(2):
            load_rows(b, b).start()
            load_idx(b, b).start()

        @pl.loop(0, per_worker, step=2)
        def _(j0):
            for b in range(2):
                j = j0 + b
                load_rows(j, b).wait()
                load_idx(j, b).wait()
                for k in range(n_slots):
                    scatter(b, k).start()
                for k in range(n_slots):
                    scatter(b, k).wait()

                @pl.when(j + 2 < per_worker)
                def _():
                    load_rows(j + 2, b).start()
                    load_idx(j + 2, b).start()

    return pl.kernel(
        body,
        out_type=jax.ShapeDtypeStruct((n_rows, width), xp.dtype),
        mesh=_sc_mesh(),
        scratch_types=[pltpu.VMEM((2, n_slots, SC_WINDOW), jnp.int32), pltpu.VMEM((2, SC_WINDOW, width), xp.dtype),
                       pltpu.SemaphoreType.DMA((2,)), pltpu.SemaphoreType.DMA((2,)), pltpu.SemaphoreType.DMA((2,))],
        name="moe_dispatch_rows",
    )(xp, dest_w)


def _gather_rows(yp, dest):
    n_slots, n = dest.shape
    width = yp.shape[1]
    m = n_slots * n
    per_worker = m // SC_WORKERS
    n_win = per_worker // SC_WINDOW
    assert n_win * SC_WINDOW * SC_WORKERS == m and n_win % 2 == 0

    def body(y_hbm, idx_hbm, out_hbm, idx_v, rows_v, in_sem, out_sem):
        base = _sc_worker_id() * per_worker
        pltpu.sync_copy(idx_hbm.at[pl.ds(base, per_worker)], idx_v)

        def gather(w, b):
            return pltpu.make_async_copy(y_hbm.at[idx_v.at[pl.ds(w * SC_WINDOW, SC_WINDOW)]], rows_v.at[b],
                                         in_sem.at[b])

        def put(w, b):
            return pltpu.make_async_copy(rows_v.at[b], out_hbm.at[pl.ds(base + w * SC_WINDOW, SC_WINDOW)],
                                         out_sem.at[b])

        for b in range(2):
            gather(b, b).start()

        @pl.loop(0, n_win, step=2)
        def _(w0):
            for b in range(2):
                w = w0 + b
                gather(w, b).wait()
                put(w, b).start()

                @pl.when(w >= 1)
                def _():
                    put(w - 1, 1 - b).wait()

                    @pl.when(w + 1 < n_win)
                    def _():
                        gather(w + 1, 1 - b).start()

        put(n_win - 1, 1).wait()

    out = pl.kernel(
        body,
        out_type=jax.ShapeDtypeStruct((m, width), yp.dtype),
        mesh=_sc_mesh(),
        scratch_types=[pltpu.VMEM((per_worker,), jnp.int32), pltpu.VMEM((2, SC_WINDOW, width), yp.dtype),
                       pltpu.SemaphoreType.DMA((2,)), pltpu.SemaphoreType.DMA((2,))],
        name="moe_gather_rows",
    )(yp, dest.reshape(m))
    return out.reshape(n_slots, n, width)


def _moe(x1, x1p, router_w, router_bias, w_gate, w_up, w_down, sh_gate, sh_up, sh_down, ln_g, ln_b, *, alpha,
         layer):
    n, d = x1.shape
    n_exp = router_w.shape[1]
    n_blk = (n * TOP_K + n_exp * (ROW_BLOCK - 1) + ROW_BLOCK - 1) // ROW_BLOCK
    eidx, gate, rank, cnt = _router(x1, router_w, router_bias)
    dest, blk = _tables(cnt, eidx, rank, n_blk=n_blk)
    xg = _dispatch_rows(x1p, dest, n_blk * ROW_BLOCK)
    y = _expert_ffn(blk, xg, w_gate, w_up, w_down, n_blk=n_blk, layer=layer)
    chunk = n // COMBINE_CHUNKS
    out = None
    for c in range(COMBINE_CHUNKS):
        yg = _gather_rows(y, dest[:, c * chunk:(c + 1) * chunk])
        out = _combine(x1, yg, gate, sh_gate, sh_up, sh_down, ln_g, ln_b, alpha=alpha,
                       first_row=c * chunk, into=out)
    return out


def kernel(x, a_w_in, a_conv_w, a_w_out, b_w_in, b_v_ln_g, b_v_ln_b, b_w_s, b_s_bias, b_w_out, c_w_in, c_w_grp,
           c_scale, c_w_out, d_w_in, d_conv_w, d_conv_b, d_ln_g, d_ln_b, d_w_out, ln1_g, ln1_b, ln2_g, ln2_b,
           router_w, router_bias, exp_w_gate, exp_w_up, exp_w_down, sh_w_gate, sh_w_up, sh_w_down):
    bsz, seq, d = x.shape
    depth = ln1_g.shape[0]
    n_mixers = 4
    alpha = (2 * depth) ** 0.25
    h = x.reshape(bsz * seq, d)
    for i in range(depth):
        m, j = i % n_mixers, i // n_mixers
        if m == 0:
            x1, x1p = _mixer_a(h, a_w_in[j], a_conv_w[j], a_w_out[j], ln1_g[i], ln1_b[i], seq=seq, alpha=alpha)
        elif m == 1:
            x1, x1p = _mixer_b(h, b_w_in[j], b_v_ln_g[j], b_v_ln_b[j], b_w_s[j], b_s_bias[j], b_w_out[j],
                               ln1_g[i], ln1_b[i], alpha=alpha)
        elif m == 2:
            x1, x1p = _mixer_c(h, c_w_in[j], c_w_grp[j], c_scale[j], c_w_out[j], ln1_g[i], ln1_b[i],
                               seq=seq, alpha=alpha)
        else:
            x1, x1p = _mixer_d(h, d_w_in[j], d_conv_w[j], d_conv_b[j], d_ln_g[j], d_ln_b[j], d_w_out[j],
                               ln1_g[i], ln1_b[i], seq=seq, alpha=alpha)
        h = _moe(x1, x1p, router_w[i], router_bias[i], exp_w_gate, exp_w_up, exp_w_down,
                 sh_w_gate[i], sh_w_up[i], sh_w_down[i], ln2_g[i], ln2_b[i], alpha=alpha, layer=i)
    return h.reshape(bsz, seq, d)
```
